```python
import math, functools
import jax, jax.numpy as jnp
from jax import lax
import numpy as np

D_MODEL = 1024
BATCH = 2
SEQ = 8192
DEPTH = 4
DEC_BATCH = 128
DEC_SEQ = 4
PAST_LEN = 2048
PAGE_SIZE = 128

GDN_HEADS = 4
GDN_DK = 128
GDN_DV = 128
GDN_CONV = 4
GDN_KEY = GDN_HEADS * GDN_DK
GDN_VAL = GDN_HEADS * GDN_DV
GDN_CONV_DIM = 2 * GDN_KEY + GDN_VAL
ATTN_HEADS = 8
ATTN_HD = 64
ATTN_WIDTH = ATTN_HEADS * ATTN_HD
MOBA_BLOCK = 256
MOBA_TOPK = 3
Q_BLOCK = 128
NUM_BUCKETS = 32
MAX_DISTANCE = 1024
SSM_INNER = D_MODEL
SSM_HEADDIM = 64
SSM_HEADS = SSM_INNER // SSM_HEADDIM
SSM_GROUPS = 4
SSM_DSTATE = 128
SSM_CONV = 4
SSM_CONV_DIM = SSM_INNER + 2 * SSM_GROUPS * SSM_DSTATE
CHUNK = 64
N_BRANCH = 3
D_FF = 11 * D_MODEL // 4
FFN_CONV = 3
IN_SIZES = (GDN_CONV_DIM, GDN_VAL, GDN_HEADS, GDN_HEADS,
            ATTN_WIDTH, ATTN_WIDTH, ATTN_WIDTH,
            SSM_INNER, SSM_CONV_DIM, SSM_HEADS,
            N_BRANCH * D_MODEL)
IN_DIM = sum(IN_SIZES)
EPS = 1e-6
NEG = -1e30

kernel_name = 'hybrid_gdn_moba_ssd_convffn_step'


def rms_norm(x, w):
    xf = x.astype(jnp.float32)
    y = xf * lax.rsqrt(jnp.mean(xf * xf, axis=-1, keepdims=True) + EPS)
    return (y * w.astype(jnp.float32)).astype(x.dtype)


def l2_norm(x):
    xf = x.astype(jnp.float32)
    return xf * lax.rsqrt(jnp.sum(xf * xf, axis=-1, keepdims=True) + EPS)


def causal_dwconv(x, buf, w, b=None):
    width, ch = w.shape
    xp = jnp.concatenate([buf.astype(x.dtype), x], axis=1)
    y = lax.conv_general_dilated(xp, w[:, None, :].astype(x.dtype), window_strides=(1,), padding='VALID',
                                 dimension_numbers=('NWC', 'WIO', 'NWC'), feature_group_count=ch)
    if b is not None:
        y = y + b.astype(x.dtype)
    return y, xp[:, -(width - 1):]


def t5_bucket(rel):
    n = jnp.maximum(rel, 0)
    exact = NUM_BUCKETS // 2
    nf = jnp.maximum(n, 1).astype(jnp.float32)
    large = exact + (jnp.log(nf / exact) / math.log(MAX_DISTANCE / exact) * (NUM_BUCKETS - exact)).astype(jnp.int32)
    return jnp.where(n < exact, n, jnp.minimum(large, NUM_BUCKETS - 1))


def gated_delta_rule(q, k, v, g, beta, s0):
    bsz, seq, heads, dk = q.shape
    dv = v.shape[-1]
    c = math.gcd(seq, CHUNK)
    z = seq // c

    def to_chunks(t):
        t = t.reshape((bsz, z, c, heads) + t.shape[3:])
        return jnp.moveaxis(jnp.swapaxes(t, 2, 3), 1, 0)

    qc, kc, vc = to_chunks(q), to_chunks(k), to_chunks(v)
    gc = jnp.cumsum(to_chunks(g), axis=-1)
    bc = to_chunks(beta)
    causal = jnp.tril(jnp.ones((c, c), dtype=bool))
    strict = jnp.tril(jnp.ones((c, c), dtype=bool), -1)
    gam = jnp.exp(jnp.where(causal, gc[..., :, None] - gc[..., None, :], -jnp.inf))
    kk = jnp.einsum('zbhck,zbhsk->zbhcs', kc, kc)
    a_mat = jnp.eye(c, dtype=jnp.float32) + jnp.where(strict, bc[..., :, None] * kk * gam, 0.0)
    rhs = jnp.concatenate([vc * bc[..., None], kc * (bc * jnp.exp(gc))[..., None]], axis=-1)
    sol = lax.linalg.triangular_solve(a_mat, rhs, left_side=True, lower=True, unit_diagonal=True)
    u_v, u_k = sol[..., :dv], sol[..., dv:]
    qk = jnp.einsum('zbhck,zbhsk->zbhcs', qc, kc) * gam
    q_dec = qc * jnp.exp(gc)[..., None]
    k_dec = kc * jnp.exp(gc[..., -1:] - gc)[..., None]
    c_dec = jnp.exp(gc[..., -1])

    def step(s, inp):
        u_v_z, u_k_z, qk_z, q_z, k_z, d_z = inp
        w = u_v_z - jnp.einsum('bhck,bhkv->bhcv', u_k_z, s)
        o = jnp.einsum('bhck,bhkv->bhcv', q_z, s) + jnp.einsum('bhcs,bhsv->bhcv', qk_z, w)
        s = s * d_z[..., None, None] + jnp.einsum('bhck,bhcv->bhkv', k_z, w)
        return s, o

    s, o = lax.scan(step, s0, (u_v, u_k, qk, q_dec, k_dec, c_dec))
    o = jnp.swapaxes(jnp.moveaxis(o, 0, 1), 2, 3).reshape(bsz, seq, heads, dv)
    return o, s


def ssd_scan(x, dt, a_head, bm, cm, h0):
    bsz, seq, heads, hp = x.shape
    groups, ns = bm.shape[2], bm.shape[3]
    e = heads // groups
    c = math.gcd(seq, CHUNK)
    z = seq // c

    def to_chunks(t):
        return jnp.moveaxis(t.reshape((bsz, z, c) + t.shape[2:]), 1, 0)

    la = to_chunks((dt * a_head).reshape(bsz, seq, groups, e))
    xdt = to_chunks((x * dt[..., None]).reshape(bsz, seq, groups, e, hp))
    bc, cc = to_chunks(bm), to_chunks(cm)
    acum = jnp.cumsum(la, axis=2)
    causal = jnp.tril(jnp.ones((c, c), dtype=bool))[:, :, None, None]
    seg = jnp.exp(jnp.where(causal, acum[:, :, :, None] - acum[:, :, None, :], -jnp.inf))
    cb = jnp.einsum('zbcgn,zbsgn->zbcsg', cc, bc)
    y_intra = jnp.einsum('zbcsge,zbsgep->zbcgep', cb[..., None] * seg, xdt)
    states = jnp.einsum('zbsgn,zbsgep->zbgepn', bc, xdt * jnp.exp(acum[:, :, -1:] - acum)[..., None])
    c_dec = jnp.exp(acum[:, :, -1])
    c_in = jnp.exp(acum)

    def step(h, inp):
        st, cd, cz, ci = inp
        y = jnp.einsum('bcgn,bgepn->bcgep', cz, h) * ci[..., None]
        return h * cd[..., None, None] + st, y

    h, y_inter = lax.scan(step, h0.reshape(bsz, groups, e, hp, ns), (states, c_dec, cc, c_in))
    y = jnp.moveaxis(y_intra + y_inter, 0, 1).reshape(bsz, seq, heads, hp)
    return y, h.reshape(bsz, heads, hp, ns)


def moba_attend(q, qpos, n_past, kmean, kb, vb, own_k, own_v, own_pos, topk, rel_bias):
    bsz, nq, heads, hd = q.shape
    f32 = jnp.float32
    scale = hd ** -0.5
    qf = q.astype(f32)
    bias_tab = rel_bias.astype(f32)
    rel_own = qpos[:, None] - own_pos[None, :]
    lo_own = jnp.einsum('bqhd,bkhd->bqhk', qf, own_k.astype(f32)) * scale
    lo_own = lo_own + jnp.swapaxes(bias_tab[t5_bucket(rel_own)], 1, 2)
    lo_own = jnp.where((rel_own >= 0)[:, None, :], lo_own, NEG)
    if topk == 0:
        prob = jax.nn.softmax(lo_own, axis=-1)
        return jnp.einsum('bqhk,bkhd->bqhd', prob, own_v.astype(f32)).astype(q.dtype)
    n_blocks = kmean.shape[1]
    blk = kb.shape[3]
    blk_score = jnp.einsum('bqhd,bnhd->bqhn', qf, kmean)
    blk_score = jnp.where(jnp.arange(n_blocks) < n_past, blk_score, NEG)
    _, idx = lax.top_k(blk_score, topk)
    b_i = jnp.arange(bsz)[:, None, None, None]
    h_i = jnp.arange(heads)[None, None, :, None]
    k_sel = kb[b_i, h_i, idx]
    v_sel = vb[b_i, h_i, idx]
    lo_sel = jnp.einsum('bqhd,bqhkld->bqhkl', qf, k_sel.astype(f32)) * scale
    k_pos = idx[..., None] * blk + jnp.arange(blk)
    bucket = t5_bucket(qpos[None, :, None, None, None] - k_pos)
    lo_sel = lo_sel + bias_tab[bucket, h_i[..., None]]
    lo_sel = jnp.where((idx < n_past)[..., None], lo_sel, NEG)
    logits = jnp.concatenate([lo_sel.reshape(bsz, nq, heads, topk * blk), lo_own], axis=-1)
    prob = jax.nn.softmax(logits, axis=-1)
    p_sel = prob[..., :topk * blk].reshape(bsz, nq, heads, topk, blk)
    p_own = prob[..., topk * blk:]
    out = (jnp.einsum('bqhkl,bqhkld->bqhd', p_sel, v_sel.astype(f32))
           + jnp.einsum('bqhk,bkhd->bqhd', p_own, own_v.astype(f32)))
    return out.astype(q.dtype)


def moba_prompt(q, k, v, rel_bias):
    bsz, seq, heads, hd = q.shape
    n_blocks = -(-seq // MOBA_BLOCK)
    pad = n_blocks * MOBA_BLOCK - seq
    k_pad = jnp.pad(k, ((0, 0), (0, pad), (0, 0), (0, 0)))
    v_pad = jnp.pad(v, ((0, 0), (0, pad), (0, 0), (0, 0)))
    kb = k_pad.reshape(bsz, n_blocks, MOBA_BLOCK, heads, hd)
    kmean = jnp.mean(kb.astype(jnp.float32), axis=2)
    kb_h = kb.transpose(0, 3, 1, 2, 4)
    vb_h = v_pad.reshape(bsz, n_blocks, MOBA_BLOCK, heads, hd).transpose(0, 3, 1, 2, 4)
    topk = min(MOBA_TOPK, n_blocks - 1)

    def query_block(i):
        q0 = i * Q_BLOCK
        j = q0 // MOBA_BLOCK
        q_blk = lax.dynamic_slice_in_dim(q, q0, Q_BLOCK, axis=1)
        own_k = lax.dynamic_slice_in_dim(k_pad, j * MOBA_BLOCK, MOBA_BLOCK, axis=1)
        own_v = lax.dynamic_slice_in_dim(v_pad, j * MOBA_BLOCK, MOBA_BLOCK, axis=1)
        return moba_attend(q_blk, q0 + jnp.arange(Q_BLOCK), j, kmean, kb_h, vb_h, own_k, own_v,
                           j * MOBA_BLOCK + jnp.arange(MOBA_BLOCK), topk, rel_bias)

    out = lax.map(query_block, jnp.arange(seq // Q_BLOCK))
    return jnp.moveaxis(out, 0, 1).reshape(bsz, seq, heads, hd)


def moba_sample(q, k, v, past_k, past_v, rel_bias):
    bsz, ns, heads, hd = q.shape
    past_len = past_k.shape[1]
    j = past_len // MOBA_BLOCK
    start = j * MOBA_BLOCK
    own_k = jnp.concatenate([past_k[:, start:].astype(k.dtype), k], axis=1)
    own_v = jnp.concatenate([past_v[:, start:].astype(v.dtype), v], axis=1)
    own_pos = start + jnp.arange(own_k.shape[1])
    qpos = past_len + jnp.arange(ns)
    kb = past_k[:, :start].reshape(bsz, j, MOBA_BLOCK, heads, hd)
    kmean = jnp.mean(kb.astype(jnp.float32), axis=2)
    kb_h = kb.transpose(0, 3, 1, 2, 4)
    vb_h = past_v[:, :start].reshape(bsz, j, MOBA_BLOCK, heads, hd).transpose(0, 3, 1, 2, 4)
    return moba_attend(q, qpos, j, kmean, kb_h, vb_h, own_k, own_v, own_pos, min(MOBA_TOPK, j), rel_bias)


def trunk_layer(x, p, state, attend):
    gdn_s, gdn_buf, ssm_h, ssm_buf, ffn_buf = state
    f32 = jnp.float32
    bsz, seq, _ = x.shape
    h = rms_norm(x, p['norm_mix'])
    proj = h @ p['w_in']
    cuts = [int(c) for c in np.cumsum(IN_SIZES)[:-1]]
    (g_qkv, g_z, g_a, g_b, a_q, a_k, a_v, s_z, s_xbc, s_dt, gate_raw) = jnp.split(proj, cuts, axis=-1)

    qkv, gdn_buf_new = causal_dwconv(g_qkv, gdn_buf, p['gdn_conv_w'])
    qkv = jax.nn.silu(qkv)
    gq = l2_norm(qkv[..., :GDN_KEY].reshape(bsz, seq, GDN_HEADS, GDN_DK)) * (GDN_DK ** -0.5)
    gk = l2_norm(qkv[..., GDN_KEY:2 * GDN_KEY].reshape(bsz, seq, GDN_HEADS, GDN_DK))
    gv = qkv[..., 2 * GDN_KEY:].reshape(bsz, seq, GDN_HEADS, GDN_DV).astype(f32)
    beta = jax.nn.sigmoid(g_b.astype(f32))
    log_decay = -jnp.exp(p['gdn_A_log'].astype(f32)) * jax.nn.softplus(g_a.astype(f32) + p['gdn_dt_bias'].astype(f32))
    go, gdn_s_new = gated_delta_rule(gq, gk, gv, log_decay, beta, gdn_s.astype(f32))
    go = rms_norm(go, p['gdn_norm']) * jax.nn.silu(g_z.reshape(bsz, seq, GDN_HEADS, GDN_DV).astype(f32))
    o_gdn = go.reshape(bsz, seq, GDN_VAL).astype(x.dtype)

    aq = rms_norm(a_q.reshape(bsz, seq, ATTN_HEADS, ATTN_HD), p['q_norm'])
    ak = rms_norm(a_k.reshape(bsz, seq, ATTN_HEADS, ATTN_HD), p['k_norm'])
    av = a_v.reshape(bsz, seq, ATTN_HEADS, ATTN_HD)
    o_att = attend(aq, ak, av).reshape(bsz, seq, ATTN_WIDTH)

    xbc, ssm_buf_new = causal_dwconv(s_xbc, ssm_buf, p['ssm_conv_w'], p['ssm_conv_b'])
    xbc = jax.nn.silu(xbc).astype(f32)
    n_bc = SSM_GROUPS * SSM_DSTATE
    sx = xbc[..., :SSM_INNER].reshape(bsz, seq, SSM_HEADS, SSM_HEADDIM)
    sb = xbc[..., SSM_INNER:SSM_INNER + n_bc].reshape(bsz, seq, SSM_GROUPS, SSM_DSTATE)
    sc = xbc[..., SSM_INNER + n_bc:].reshape(bsz, seq, SSM_GROUPS, SSM_DSTATE)
    dt = jax.nn.softplus(s_dt.astype(f32) + p['ssm_dt_bias'].astype(f32))
    a_head = -jnp.exp(p['ssm_A_log'].astype(f32))
    sy, ssm_h_new = ssd_scan(sx, dt, a_head, sb, sc, ssm_h.astype(f32))
    sy = sy + p['ssm_D'].astype(f32)[:, None] * sx
    sy = (sy.reshape(bsz, seq, SSM_INNER) * jax.nn.silu(s_z.astype(f32)))
    sy = rms_norm(sy.reshape(bsz, seq, SSM_GROUPS, SSM_INNER // SSM_GROUPS), p['ssm_norm'].reshape(SSM_GROUPS, -1))
    o_ssm = sy.reshape(bsz, seq, SSM_INNER).astype(x.dtype)

    gates = jax.nn.sigmoid(gate_raw.astype(f32)).reshape(bsz, seq, N_BRANCH, D_MODEL)
    merged = (gates[:, :, 0] * (o_gdn @ p['w_br_gdn'])
              + gates[:, :, 1] * (o_att @ p['w_br_attn'])
              + gates[:, :, 2] * (o_ssm @ p['w_br_ssm']))
    x = x + merged.astype(x.dtype) @ p['w_out']

    u = rms_norm(x, p['norm_ffn']) @ p['ffn_up']
    u, ffn_buf_new = causal_dwconv(u, ffn_buf, p['ffn_conv_w'], p['ffn_conv_b'])
    x = x + (jax.nn.silu(u[..., :D_FF]) * u[..., D_FF:]) @ p['ffn_down']
    return x, (ak, av, gdn_s_new, gdn_buf_new, ssm_h_new, ssm_buf_new, ffn_buf_new)


def setup_inputs(seed: int = 0) -> dict:
    key = jax.random.key(seed)
    ks = iter(jax.random.split(key, 48))
    f32 = jnp.float32

    def nrm(shape, scale):
        return scale * jax.random.normal(next(ks), shape, f32)

    def gain(shape):
        return 1.0 + nrm(shape, 0.02)

    def dt_bias(shape):
        dt = jnp.exp(jax.random.uniform(next(ks), shape, f32, minval=math.log(1e-3), maxval=math.log(1e-1)))
        return dt + jnp.log(-jnp.expm1(-dt))

    def a_log(shape):
        return jnp.log(jax.random.uniform(next(ks), shape, f32, minval=1.0, maxval=16.0))

    n_pages = PAST_LEN // PAGE_SIZE
    n_pool = (DEC_BATCH * n_pages * 5) // 4
    perm = jax.random.permutation(next(ks), n_pool)[:DEC_BATCH * n_pages]
    page_table = perm.reshape(DEC_BATCH, n_pages).astype(jnp.int32)
    return {
        'x_prompt': nrm((BATCH, SEQ, D_MODEL), 1.0),
        'x_sample': nrm((DEC_BATCH, DEC_SEQ, D_MODEL), 1.0),
        'cache_k': nrm((DEPTH, n_pool, PAGE_SIZE, ATTN_HEADS, ATTN_HD), 1.0),
        'cache_v': nrm((DEPTH, n_pool, PAGE_SIZE, ATTN_HEADS, ATTN_HD), 1.0),
        'page_table': page_table,
        'state_gdn': nrm((DEPTH, DEC_BATCH, GDN_HEADS, GDN_DK, GDN_DV), 0.1),
        'state_gdn_conv': nrm((DEPTH, DEC_BATCH, GDN_CONV - 1, GDN_CONV_DIM), 1.0),
        'state_ssm': nrm((DEPTH, DEC_BATCH, SSM_HEADS, SSM_HEADDIM, SSM_DSTATE), 0.1),
        'state_ssm_conv': nrm((DEPTH, DEC_BATCH, SSM_CONV - 1, SSM_CONV_DIM), 1.0),
        'state_ffn_conv': nrm((DEPTH, DEC_BATCH, FFN_CONV - 1, 2 * D_FF), 1.0),
        'rel_bias': nrm((NUM_BUCKETS, ATTN_HEADS), 0.5),
        'norm_mix': gain((DEPTH, D_MODEL)),
        'w_in': nrm((DEPTH, D_MODEL, IN_DIM), D_MODEL ** -0.5),
        'gdn_conv_w': nrm((DEPTH, GDN_CONV, GDN_CONV_DIM), GDN_CONV ** -0.5),
        'gdn_A_log': a_log((DEPTH, GDN_HEADS)),
        'gdn_dt_bias': dt_bias((DEPTH, GDN_HEADS)),
        'gdn_norm': gain((DEPTH, GDN_DV)),
        'q_norm': gain((DEPTH, ATTN_HD)),
        'k_norm': gain((DEPTH, ATTN_HD)),
        'ssm_conv_w': nrm((DEPTH, SSM_CONV, SSM_CONV_DIM), SSM_CONV ** -0.5),
        'ssm_conv_b': nrm((DEPTH, SSM_CONV_DIM), 0.02),
        'ssm_A_log': a_log((DEPTH, SSM_HEADS)),
        'ssm_dt_bias': dt_bias((DEPTH, SSM_HEADS)),
        'ssm_D': gain((DEPTH, SSM_HEADS)),
        'ssm_norm': gain((DEPTH, SSM_INNER)),
        'w_br_gdn': nrm((DEPTH, GDN_VAL, D_MODEL), GDN_VAL ** -0.5),
        'w_br_attn': nrm((DEPTH, ATTN_WIDTH, D_MODEL), ATTN_WIDTH ** -0.5),
        'w_br_ssm': nrm((DEPTH, SSM_INNER, D_MODEL), SSM_INNER ** -0.5),
        'w_out': nrm((DEPTH, D_MODEL, D_MODEL), D_MODEL ** -0.5),
        'norm_ffn': gain((DEPTH, D_MODEL)),
        'ffn_up': nrm((DEPTH, D_MODEL, 2 * D_FF), D_MODEL ** -0.5),
        'ffn_conv_w': nrm((DEPTH, FFN_CONV, 2 * D_FF), FFN_CONV ** -0.5),
        'ffn_conv_b': nrm((DEPTH, 2 * D_FF), 0.02),
        'ffn_down': nrm((DEPTH, D_FF, D_MODEL), D_FF ** -0.5),
    }


def reference(x_prompt, x_sample, cache_k, cache_v, page_table, state_gdn, state_gdn_conv, state_ssm,
              state_ssm_conv, state_ffn_conv, rel_bias, norm_mix, w_in, gdn_conv_w, gdn_A_log, gdn_dt_bias,
              gdn_norm, q_norm, k_norm, ssm_conv_w, ssm_conv_b, ssm_A_log, ssm_dt_bias, ssm_D, ssm_norm,
              w_br_gdn, w_br_attn, w_br_ssm, w_out, norm_ffn, ffn_up, ffn_conv_w, ffn_conv_b, ffn_down):
    f32 = jnp.float32
    bp = x_prompt.shape[0]
    bd = x_sample.shape[0]
    past_len = page_table.shape[1] * cache_k.shape[2]
    prompt_state = (
        jnp.zeros((bp, GDN_HEADS, GDN_DK, GDN_DV), f32),
        jnp.zeros((bp, GDN_CONV - 1, GDN_CONV_DIM), x_prompt.dtype),
        jnp.zeros((bp, SSM_HEADS, SSM_HEADDIM, SSM_DSTATE), f32),
        jnp.zeros((bp, SSM_CONV - 1, SSM_CONV_DIM), x_prompt.dtype),
        jnp.zeros((bp, FFN_CONV - 1, 2 * D_FF), x_prompt.dtype),
    )
    attend_prompt = functools.partial(moba_prompt, rel_bias=rel_bias)
    yp, ys = x_prompt, x_sample
    out_p = [[] for _ in range(7)]
    out_s = [[] for _ in range(7)]
    for l in range(DEPTH):
        p = {
            'norm_mix': norm_mix[l], 'w_in': w_in[l], 'gdn_conv_w': gdn_conv_w[l],
            'gdn_A_log': gdn_A_log[l], 'gdn_dt_bias': gdn_dt_bias[l], 'gdn_norm': gdn_norm[l],
            'q_norm': q_norm[l], 'k_norm': k_norm[l], 'ssm_conv_w': ssm_conv_w[l], 'ssm_conv_b': ssm_conv_b[l],
            'ssm_A_log': ssm_A_log[l], 'ssm_dt_bias': ssm_dt_bias[l], 'ssm_D': ssm_D[l], 'ssm_norm': ssm_norm[l],
            'w_br_gdn': w_br_gdn[l], 'w_br_attn': w_br_attn[l], 'w_br_ssm': w_br_ssm[l], 'w_out': w_out[l],
            'norm_ffn': norm_ffn[l], 'ffn_up': ffn_up[l], 'ffn_conv_w': ffn_conv_w[l],
            'ffn_conv_b': ffn_conv_b[l], 'ffn_down': ffn_down[l],
        }
        yp, new_p = trunk_layer(yp, p, prompt_state, attend_prompt)
        past_k = cache_k[l][page_table].reshape(bd, past_len, ATTN_HEADS, ATTN_HD)
        past_v = cache_v[l][page_table].reshape(bd, past_len, ATTN_HEADS, ATTN_HD)
        attend_sample = functools.partial(moba_sample, past_k=past_k, past_v=past_v, rel_bias=rel_bias)
        sample_state = (state_gdn[l], state_gdn_conv[l], state_ssm[l], state_ssm_conv[l], state_ffn_conv[l])
        ys, new_s = trunk_layer(ys, p, sample_state, attend_sample)
        for acc, t in zip(out_p, new_p):
            acc.append(t)
        for acc, t in zip(out_s, new_s):
            acc.append(t)
    k_p, v_p, gdn_p, gdn_conv_p, ssm_p, ssm_conv_p, ffn_conv_p = [jnp.stack(a) for a in out_p]
    k_s, v_s, gdn_s, gdn_conv_s, ssm_s, ssm_conv_s, ffn_conv_s = [jnp.stack(a) for a in out_s]
    return (yp, ys, k_p, v_p, gdn_p, gdn_conv_p, ssm_p, ssm_conv_p, ffn_conv_p,
            k_s, v_s, gdn_s, gdn_conv_s, ssm_s, ssm_conv_s, ffn_conv_s)
```

```python
import functools
import math

import numpy as np
import jax
import jax.numpy as jnp
from jax import lax
from jax.experimental import pallas as pl
from jax.experimental.pallas import tpu as pltpu

F32 = jnp.float32
BF16 = jnp.bfloat16
HI = lax.Precision.HIGHEST

D_MODEL = 1024
GDN_HEADS, GDN_DK, GDN_DV, GDN_CONV = 4, 128, 128, 4
GDN_KEY = GDN_HEADS * GDN_DK
GDN_VAL = GDN_HEADS * GDN_DV
GDN_CONV_DIM = 2 * GDN_KEY + GDN_VAL
ATTN_HEADS, ATTN_HD = 8, 64
ATTN_WIDTH = ATTN_HEADS * ATTN_HD
MOBA_BLOCK, MOBA_TOPK = 256, 3
NUM_BUCKETS, MAX_DISTANCE = 32, 1024
SSM_INNER, SSM_HEADDIM, SSM_GROUPS, SSM_DSTATE, SSM_CONV = 1024, 64, 4, 128, 4
SSM_HEADS = SSM_INNER // SSM_HEADDIM
SSM_CONV_DIM = SSM_INNER + 2 * SSM_GROUPS * SSM_DSTATE
N_BRANCH = 3
D_FF = 11 * D_MODEL // 4
FFN_CONV = 3
IN_SIZES = (GDN_CONV_DIM, GDN_VAL, GDN_HEADS, GDN_HEADS, ATTN_WIDTH, ATTN_WIDTH, ATTN_WIDTH,
            SSM_INNER, SSM_CONV_DIM, SSM_HEADS, N_BRANCH * D_MODEL)
EPS = 1e-6
NEG = -1e30
LOWEST = -3e38

LANES = 128
SUBLANES = 8
VMEM_LIMIT = 56 * 1024 * 1024

COL_SZ, COL_SX, COL_G0, COL_G1, COL_G2 = 0, 1024, 2048, 3072, 4096
COL_GQ, COL_GK, COL_GV, COL_GZ = 5120, 5632, 6144, 6656
COL_AQ, COL_AK, COL_AV = 7168, 7680, 8192
COL_SB, COL_SC = 8704, 9216
COL_SMALL = 9728
P_WIDTH = COL_SMALL + LANES
SM_A, SM_B, SM_DT = 0, GDN_HEADS, 2 * GDN_HEADS
FFN_TN = 1408
BIAS_ND = 6


def _cparams(sem):
    return pltpu.CompilerParams(dimension_semantics=sem, vmem_limit_bytes=VMEM_LIMIT)


def _dot(a, b, prec=None):
    return lax.dot_general(a, b, (((1,), (0,)), ((), ())), precision=prec, preferred_element_type=F32)


def _dot_nt(a, b, prec=None):
    return lax.dot_general(a, b, (((1,), (1,)), ((), ())), precision=prec, preferred_element_type=F32)


def _dot_tn(a, b, prec=None):
    return lax.dot_general(a, b, (((0,), (0,)), ((), ())), precision=prec, preferred_element_type=F32)


def _sigmoid(x):
    return 1.0 / (1.0 + jnp.exp(-x))


def _silu(x):
    return x * _sigmoid(x)


def _softplus(x):
    return jnp.maximum(x, 0.0) + jnp.log1p(jnp.exp(-jnp.abs(x)))


def _rms(x, w):
    return x * lax.rsqrt(jnp.mean(x * x, axis=-1, keepdims=True) + EPS) * w


def _norm_proj_kernel(x_ref, nw_ref, w_ref, o_ref, h_ref):
    @pl.when(pl.program_id(1) == 0)
    def _():
        h_ref[...] = _rms(x_ref[...], nw_ref[...]).astype(BF16)

    o_ref[...] = _dot(h_ref[...], w_ref[...])


def _norm_proj(x, nw, w, tm):
    m = x.shape[0]
    tn = 1408
    return pl.pallas_call(
        _norm_proj_kernel,
        grid=(m // tm, P_WIDTH // tn),
        in_specs=[pl.BlockSpec((tm, D_MODEL), lambda i, j: (i, 0)),
                  pl.BlockSpec((1, D_MODEL), lambda i, j: (0, 0)),
                  pl.BlockSpec((D_MODEL, tn), lambda i, j: (0, j))],
        out_specs=pl.BlockSpec((tm, tn), lambda i, j: (i, j)),
        out_shape=jax.ShapeDtypeStruct((m, P_WIDTH), F32),
        scratch_shapes=[pltpu.VMEM((tm, D_MODEL), BF16)],
        compiler_params=_cparams(("parallel", "arbitrary")),
        name="norm_proj",
    )(x, nw, w)


def _conv_from_ext(ext_ref, g, cw, c, width):
    acc = None
    for k in range(width):
        off = SUBLANES - (width - 1) + k
        term = cw[k:k + 1, :] * ext_ref[g, off:off + c, :]
        acc = term if acc is None else acc + term
    return acc


def _tri_masks(c):
    r = lax.broadcasted_iota(jnp.int32, (c, c), 0)
    s = lax.broadcasted_iota(jnp.int32, (c, c), 1)
    return r >= s, r > s


def _unit_lower_inverse(n_strict, c):
    r = lax.broadcasted_iota(jnp.int32, (c, c), 0)
    s = lax.broadcasted_iota(jnp.int32, (c, c), 1)
    eye = jnp.where(r == s, 1.0, 0.0).astype(F32)
    neg = -n_strict
    t = eye + neg
    p = neg
    span = 2
    while span < c:
        p = _dot(p, p, HI)
        t = t + _dot(t, p, HI)
        span *= 2
    return t


def _gdn_kernel(q_ref, k_ref, v_ref, z_ref, sm_ref, cw_ref, par_ref, nw_ref, cs_ref, s0_ref,
                o_ref, so_ref, co_ref, ext_ref, s_ref, *, G, C, L, NC):
    ci = pl.program_id(1)
    hist = GDN_CONV - 1

    @pl.when(ci == 0)
    def _():
        s_ref[...] = s0_ref[...]
        ext_ref[...] = jnp.zeros(ext_ref.shape, F32)
        for g in range(G):
            ext_ref[g, SUBLANES - hist:SUBLANES, :] = cs_ref[g]

    causal, strict = _tri_masks(C)
    tril = jnp.where(causal, 1.0, 0.0).astype(F32)
    triu = jnp.where(lax.broadcasted_iota(jnp.int32, (C, C), 0)
                     <= lax.broadcasted_iota(jnp.int32, (C, C), 1), 1.0, 0.0).astype(F32)
    rows = lax.broadcasted_iota(jnp.int32, (C, 1), 0)
    valid = rows < L
    lane = lax.broadcasted_iota(jnp.int32, (1, LANES), 1)
    dt_bias = par_ref[0:1, :]
    neg_a = -jnp.exp(par_ref[1:2, :])
    cw = cw_ref[...]
    nw = nw_ref[...]

    for g in range(G):
        ext_ref[g, SUBLANES:SUBLANES + C, 0:GDN_KEY] = q_ref[g]
        ext_ref[g, SUBLANES:SUBLANES + C, GDN_KEY:2 * GDN_KEY] = k_ref[g]
        ext_ref[g, SUBLANES:SUBLANES + C, 2 * GDN_KEY:] = v_ref[g]
        xc = _silu(_conv_from_ext(ext_ref, g, cw, C, GDN_CONV))
        new_hist = ext_ref[g, SUBLANES + L - hist:SUBLANES + L, :]
        co_ref[g] = new_hist
        ext_ref[g, SUBLANES - hist:SUBLANES, :] = new_hist

        small = sm_ref[g]
        gmat = jnp.where(valid & (lane < SM_B), neg_a * _softplus(small + dt_bias), 0.0)
        bmat = jnp.where(valid, _sigmoid(small), 0.0)
        gc = _dot(tril, gmat, HI)
        gct = _dot_tn(gmat, triu, HI)
        z = z_ref[g]

        for h in range(GDN_HEADS):
            sl = slice(h * GDN_DK, (h + 1) * GDN_DK)
            gcol = gc[:, SM_A + h:SM_A + h + 1]
            grow = gct[SM_A + h:SM_A + h + 1, :]
            glast = gc[C - 1:C, SM_A + h:SM_A + h + 1]
            bcol = bmat[:, SM_B + h:SM_B + h + 1]
            gam = jnp.where(causal, jnp.exp(jnp.minimum(gcol - grow, 0.0)), 0.0)
            qh = xc[:, sl]
            kh = xc[:, GDN_KEY + h * GDN_DK:GDN_KEY + (h + 1) * GDN_DK]
            vh = xc[:, 2 * GDN_KEY + h * GDN_DV:2 * GDN_KEY + (h + 1) * GDN_DV]
            qh = qh * lax.rsqrt(jnp.sum(qh * qh, axis=-1, keepdims=True) + EPS) * (GDN_DK ** -0.5)
            kh = jnp.where(valid, kh * lax.rsqrt(jnp.sum(kh * kh, axis=-1, keepdims=True) + EPS), 0.0)
            vh = jnp.where(valid, vh, 0.0)
            kk = _dot_nt(kh, kh, HI)
            tinv = _unit_lower_inverse(jnp.where(strict, bcol * kk * gam, 0.0), C)
            ecol = jnp.exp(gcol)
            rhs = jnp.concatenate([vh * bcol, kh * (bcol * ecol)], axis=1)
            u = _dot(tinv, rhs, HI)
            u_v, u_k = u[:, :GDN_DV], u[:, GDN_DV:]
            qk = _dot_nt(qh, kh, HI) * gam
            q_dec = qh * ecol
            k_dec = kh * jnp.exp(glast - gcol)
            s = s_ref[g, h]
            w = u_v - _dot(u_k, s, HI)
            o = _dot(q_dec, s, HI) + _dot(qk, w, HI)
            s_ref[g, h] = s * jnp.exp(glast) + _dot_tn(k_dec, w, HI)
            o = _rms(o, nw) * _silu(z[:, h * GDN_DV:(h + 1) * GDN_DV])
            o_ref[g, :, h * GDN_DV:(h + 1) * GDN_DV] = o.astype(o_ref.dtype)

    @pl.when(ci == NC - 1)
    def _():
        so_ref[...] = s_ref[...]


def _gdn(p3, conv_w, par, nw, conv_state, state, layer, *, G, C, L):
    b, t, _ = p3.shape
    nc = t // C
    blk = lambda col, w: pl.BlockSpec((G, C, w), lambda i, c: (i, c, col // w))
    fixed = lambda shape: pl.BlockSpec(shape, lambda i, c: (0,) * len(shape))
    kern = functools.partial(_gdn_kernel, G=G, C=C, L=L, NC=nc)
    return pl.pallas_call(
        kern,
        grid=(b // G, nc),
        in_specs=[blk(COL_GQ, GDN_KEY), blk(COL_GK, GDN_KEY), blk(COL_GV, GDN_VAL), blk(COL_GZ, GDN_VAL),
                  blk(COL_SMALL, LANES),
                  fixed((GDN_CONV, GDN_CONV_DIM)), fixed((SUBLANES, LANES)), fixed((1, GDN_DV)),
                  pl.BlockSpec((None, G, GDN_CONV - 1, GDN_CONV_DIM), lambda i, c: (layer, i, 0, 0)),
                  pl.BlockSpec((None, G, GDN_HEADS, GDN_DK, GDN_DV), lambda i, c: (layer, i, 0, 0, 0))],
        out_specs=[pl.BlockSpec((G, C, GDN_VAL), lambda i, c: (i, c, 0)),
                   pl.BlockSpec((G, GDN_HEADS, GDN_DK, GDN_DV), lambda i, c: (i, 0, 0, 0)),
                   pl.BlockSpec((G, GDN_CONV - 1, GDN_CONV_DIM), lambda i, c: (i, 0, 0))],
        out_shape=[jax.ShapeDtypeStruct((b, t, GDN_VAL), F32),
                   jax.ShapeDtypeStruct((b, GDN_HEADS, GDN_DK, GDN_DV), F32),
                   jax.ShapeDtypeStruct((b, GDN_CONV - 1, GDN_CONV_DIM), F32)],
        scratch_shapes=[pltpu.VMEM((G, C + SUBLANES, GDN_CONV_DIM), F32),
                        pltpu.VMEM((G, GDN_HEADS, GDN_DK, GDN_DV), F32)],
        compiler_params=_cparams(("parallel", "arbitrary")),
        name="gdn",
    )(p3, p3, p3, p3, p3, conv_w, par, nw, conv_state, state)


def _ssd_kernel(x_ref, b_ref, c_ref, z_ref, sm_ref, cw_ref, cb_ref, par_ref, dexp_ref, nw_ref, e_ref,
                cs_ref, h0_ref, o_ref, ho_ref, co_ref, ext_ref, h_ref, y_ref, *, G, C, L, NC):
    ci = pl.program_id(1)
    hist = SSM_CONV - 1
    nbc = SSM_GROUPS * SSM_DSTATE

    @pl.when(ci == 0)
    def _():
        h_ref[...] = h0_ref[...].reshape(h_ref.shape)
        ext_ref[...] = jnp.zeros(ext_ref.shape, F32)
        for g in range(G):
            ext_ref[g, SUBLANES - hist:SUBLANES, :] = cs_ref[g]

    causal, _ = _tri_masks(C)
    tril = jnp.where(causal, 1.0, 0.0).astype(F32)
    triu = jnp.where(lax.broadcasted_iota(jnp.int32, (C, C), 0)
                     <= lax.broadcasted_iota(jnp.int32, (C, C), 1), 1.0, 0.0).astype(F32)
    rows = lax.broadcasted_iota(jnp.int32, (C, 1), 0)
    valid = rows < L
    lane = lax.broadcasted_iota(jnp.int32, (1, LANES), 1)
    dt_lanes = (lane >= SM_DT) & (lane < SM_DT + SSM_HEADS)
    half = [jnp.where(lane < SSM_HEADDIM, 1.0, 0.0).astype(F32), jnp.where(lane >= SSM_HEADDIM, 1.0, 0.0).astype(F32)]
    dt_bias = par_ref[0:1, :]
    neg_a = -jnp.exp(par_ref[1:2, :])
    cw = cw_ref[...]
    cb = cb_ref[...]
    expand = e_ref[...]

    for g in range(G):
        ext_ref[g, SUBLANES:SUBLANES + C, 0:SSM_INNER] = x_ref[g]
        ext_ref[g, SUBLANES:SUBLANES + C, SSM_INNER:SSM_INNER + nbc] = b_ref[g]
        ext_ref[g, SUBLANES:SUBLANES + C, SSM_INNER + nbc:] = c_ref[g]
        xbc = _silu(_conv_from_ext(ext_ref, g, cw, C, SSM_CONV) + cb)
        new_hist = ext_ref[g, SUBLANES + L - hist:SUBLANES + L, :]
        co_ref[g] = new_hist
        ext_ref[g, SUBLANES - hist:SUBLANES, :] = new_hist

        dt = jnp.where(valid & dt_lanes, _softplus(sm_ref[g] + dt_bias), 0.0)
        la_exp = _dot(dt * neg_a, expand, HI)
        dt_exp = _dot(dt, expand, HI)
        acum = _dot(tril, la_exp, HI)
        acum_t = _dot_tn(la_exp, triu, HI)
        atot_row = acum[C - 1:C, :]
        atot_col = acum_t[:, C - 1:C]
        x = xbc[:, :SSM_INNER]
        xdt = x * dt_exp
        xdec = xdt * jnp.exp(atot_row - acum)
        cin = jnp.exp(acum)

        for gi in range(SSM_GROUPS):
            bg = xbc[:, SSM_INNER + gi * SSM_DSTATE:SSM_INNER + (gi + 1) * SSM_DSTATE]
            cg = xbc[:, SSM_INNER + nbc + gi * SSM_DSTATE:SSM_INNER + nbc + (gi + 1) * SSM_DSTATE]
            cbm = _dot_nt(cg, bg, HI)
            for pr in range(2):
                p = gi * 2 + pr
                ps = slice(p * LANES, (p + 1) * LANES)
                xdt_p = xdt[:, ps]
                y_p = None
                for e in range(2):
                    head = 2 * p + e
                    acol = acum[:, head * SSM_HEADDIM:head * SSM_HEADDIM + 1]
                    arow = acum_t[head * SSM_HEADDIM:head * SSM_HEADDIM + 1, :]
                    seg = jnp.where(causal, jnp.exp(jnp.minimum(acol - arow, 0.0)), 0.0)
                    term = _dot(cbm * seg, xdt_p * half[e], HI)
                    y_p = term if y_p is None else y_p + term
                hp = h_ref[g, p]
                y_p = y_p + _dot_nt(cg, hp, HI) * cin[:, ps]
                h_ref[g, p] = hp * jnp.exp(atot_col[p * LANES:(p + 1) * LANES, :]) + _dot_tn(xdec[:, ps], bg, HI)
                y_ref[:, ps] = y_p

        y = (y_ref[...] + dexp_ref[...] * x) * _silu(z_ref[g])
        gw = SSM_INNER // SSM_GROUPS
        for gi in range(SSM_GROUPS):
            gs = slice(gi * gw, (gi + 1) * gw)
            o_ref[g, :, gs] = _rms(y[:, gs], nw_ref[:, gs]).astype(o_ref.dtype)

    @pl.when(ci == NC - 1)
    def _():
        ho_ref[...] = h_ref[...].reshape(ho_ref.shape)


def _ssd(p3, conv_w, conv_b, par, dexp, nw, expand, conv_state, state, layer, *, G, C, L):
    b, t, _ = p3.shape
    nc = t // C
    nbc = SSM_GROUPS * SSM_DSTATE
    blk = lambda col, w: pl.BlockSpec((G, C, w), lambda i, c: (i, c, col // w))
    fixed = lambda shape: pl.BlockSpec(shape, lambda i, c: (0,) * len(shape))
    kern = functools.partial(_ssd_kernel, G=G, C=C, L=L, NC=nc)
    return pl.pallas_call(
        kern,
        grid=(b // G, nc),
        in_specs=[blk(COL_SX, SSM_INNER), blk(COL_SB, nbc), blk(COL_SC, nbc), blk(COL_SZ, SSM_INNER),
                  blk(COL_SMALL, LANES),
                  fixed((SSM_CONV, SSM_CONV_DIM)), fixed((1, SSM_CONV_DIM)), fixed((SUBLANES, LANES)),
                  fixed((1, SSM_INNER)), fixed((1, SSM_INNER)), fixed((LANES, SSM_INNER)),
                  pl.BlockSpec((None, G, SSM_CONV - 1, SSM_CONV_DIM), lambda i, c: (layer, i, 0, 0)),
                  pl.BlockSpec((None, G, SSM_HEADS, SSM_HEADDIM, SSM_DSTATE), lambda i, c: (layer, i, 0, 0, 0))],
        out_specs=[pl.BlockSpec((G, C, SSM_INNER), lambda i, c: (i, c, 0)),
                   pl.BlockSpec((G, SSM_HEADS, SSM_HEADDIM, SSM_DSTATE), lambda i, c: (i, 0, 0, 0)),
                   pl.BlockSpec((G, SSM_CONV - 1, SSM_CONV_DIM), lambda i, c: (i, 0, 0))],
        out_shape=[jax.ShapeDtypeStruct((b, t, SSM_INNER), F32),
                   jax.ShapeDtypeStruct((b, SSM_HEADS, SSM_HEADDIM, SSM_DSTATE), F32),
                   jax.ShapeDtypeStruct((b, SSM_CONV - 1, SSM_CONV_DIM), F32)],
        scratch_shapes=[pltpu.VMEM((G, C + SUBLANES, SSM_CONV_DIM), F32),
                        pltpu.VMEM((G, SSM_HEADS // 2, LANES, SSM_DSTATE), F32),
                        pltpu.VMEM((C, SSM_INNER), F32)],
        compiler_params=_cparams(("parallel", "arbitrary")),
        name="ssd",
    )(p3, p3, p3, p3, p3, conv_w, conv_b, par, dexp, nw, expand, conv_state, state)


def _split_dot(a, b_bf16):
    hi = a.astype(BF16)
    lo = (a - hi.astype(F32)).astype(BF16)
    return _dot(hi, b_bf16) + _dot(lo, b_bf16)


def _qknorm_kernel(q_ref, k_ref, qw_ref, kw_ref, ones_ref, oq_ref, ok_ref, km_ref):
    ones = ones_ref[...]
    q = q_ref[...]
    k = k_ref[...]
    inv = 1.0 / ATTN_HD
    oq_ref[...] = q * lax.rsqrt(_split_dot(q * q, ones) * inv + EPS) * qw_ref[...]
    kn = k * lax.rsqrt(_split_dot(k * k, ones) * inv + EPS) * kw_ref[...]
    ok_ref[...] = kn
    km_ref[0] = jnp.mean(kn, axis=0, keepdims=True)


def _qknorm(p2, qw, kw, ones, tm):
    m = p2.shape[0]
    nt = m // tm
    return pl.pallas_call(
        _qknorm_kernel,
        grid=(nt,),
        in_specs=[pl.BlockSpec((tm, ATTN_WIDTH), lambda i: (i, COL_AQ // ATTN_WIDTH)),
                  pl.BlockSpec((tm, ATTN_WIDTH), lambda i: (i, COL_AK // ATTN_WIDTH)),
                  pl.BlockSpec((1, ATTN_WIDTH), lambda i: (0, 0)),
                  pl.BlockSpec((1, ATTN_WIDTH), lambda i: (0, 0)),
                  pl.BlockSpec((ATTN_WIDTH, ATTN_WIDTH), lambda i: (0, 0))],
        out_specs=[pl.BlockSpec((tm, ATTN_WIDTH), lambda i: (i, 0)),
                   pl.BlockSpec((tm, ATTN_WIDTH), lambda i: (i, 0)),
                   pl.BlockSpec((1, 1, ATTN_WIDTH), lambda i: (i, 0, 0))],
        out_shape=[jax.ShapeDtypeStruct((m, ATTN_WIDTH), F32),
                   jax.ShapeDtypeStruct((m, ATTN_WIDTH), F32),
                   jax.ShapeDtypeStruct((nt, 1, ATTN_WIDTH), F32)],
        compiler_params=_cparams(("parallel",)),
        name="qknorm",
    )(p2, p2, qw, kw, ones)


def _top_blocks(sc, n_valid, lane):
    sc = jnp.where(lane < n_valid, sc, NEG)
    sel = jnp.zeros(sc.shape, F32)
    for _ in range(MOBA_TOPK):
        mx = jnp.max(sc, axis=-1, keepdims=True)
        idx = jnp.min(jnp.where(sc == mx, lane, LANES), axis=-1, keepdims=True)
        pick = lane == idx
        sel = jnp.where(pick, 1.0, sel)
        sc = jnp.where(pick, LOWEST, sc)
    return jnp.where(lane < n_valid, sel, 0.0)


def _moba_kernel(jq_tab, n_tab, q_ref, k_ref, v_ref, km_ref, bias_ref, o_ref,
                 qm_ref, sel_ref, m_ref, l_ref, acc_ref):
    s = pl.program_id(1)
    jq = jq_tab[s]
    n = n_tab[s]
    tq = MOBA_BLOCK
    lane = lax.broadcasted_iota(jnp.int32, (1, LANES), 1)
    half = [jnp.where(lane < ATTN_HD, 1.0, 0.0).astype(F32), jnp.where(lane >= ATTN_HD, 1.0, 0.0).astype(F32)]

    @pl.when(n == 0)
    def _():
        q = q_ref[0] * (ATTN_HD ** -0.5)
        km = km_ref[0]
        for h in range(ATTN_HEADS):
            ps = slice((h // 2) * LANES, (h // 2 + 1) * LANES)
            qm = q[:, ps] * half[h % 2]
            qm_ref[h] = qm
            sel_ref[h] = _top_blocks(_dot_nt(qm, km[:, ps], HI), jq, lane)
        m_ref[...] = jnp.full(m_ref.shape, NEG, F32)
        l_ref[...] = jnp.zeros(l_ref.shape, F32)
        acc_ref[...] = jnp.zeros(acc_ref.shape, F32)

    d = jq - n
    dd = jnp.minimum(d, BIAS_ND - 1)
    own = d == 0
    r = lax.broadcasted_iota(jnp.int32, (tq, MOBA_BLOCK), 0)
    c = lax.broadcasted_iota(jnp.int32, (tq, MOBA_BLOCK), 1)
    causal = jnp.where(r >= c, 1.0, 0.0).astype(F32)
    kb = k_ref[0].astype(BF16)
    vb = v_ref[0]
    for h in range(ATTN_HEADS):
        ps = slice((h // 2) * LANES, (h // 2 + 1) * LANES)
        lg = _dot_nt(qm_ref[h].astype(BF16), kb[:, ps]) + bias_ref[h, dd]
        rowsel = jnp.sum(jnp.where(lane == n, sel_ref[h], 0.0), axis=-1, keepdims=True)
        ok = jnp.where(own, causal, rowsel) > 0.0
        lg = jnp.where(ok, lg, NEG)
        m_old = m_ref[h]
        m_new = jnp.maximum(m_old, jnp.max(lg, axis=-1, keepdims=True))
        alpha = jnp.exp(m_old - m_new)
        pr = jnp.where(ok, jnp.exp(lg - m_new[:, 0:1]), 0.0)
        l_ref[h] = alpha * l_ref[h] + jnp.sum(pr, axis=-1, keepdims=True)
        m_ref[h] = m_new
        vm = (vb[:, ps] * half[h % 2]).astype(BF16)
        acc_ref[h] = acc_ref[h] * alpha + _dot(pr.astype(BF16), vm)

    @pl.when(d == 0)
    def _():
        for p in range(ATTN_HEADS // 2):
            o = acc_ref[2 * p] / l_ref[2 * p] + acc_ref[2 * p + 1] / l_ref[2 * p + 1]
            o_ref[0, :, p * LANES:(p + 1) * LANES] = o.astype(o_ref.dtype)


def _moba_prompt(aq, ak, av_p3, kmean, bias_tiles):
    b, t, _ = aq.shape
    nb = t // MOBA_BLOCK
    jq_tab = np.concatenate([np.full(j + 1, j) for j in range(nb)]).astype(np.int32)
    n_tab = np.concatenate([np.arange(j + 1) for j in range(nb)]).astype(np.int32)
    nsteps = int(jq_tab.shape[0])
    gs = pltpu.PrefetchScalarGridSpec(
        num_scalar_prefetch=2,
        grid=(b, nsteps),
        in_specs=[pl.BlockSpec((1, MOBA_BLOCK, ATTN_WIDTH), lambda i, s, jq, nn: (i, jq[s], 0)),
                  pl.BlockSpec((1, MOBA_BLOCK, ATTN_WIDTH), lambda i, s, jq, nn: (i, nn[s], 0)),
                  pl.BlockSpec((1, MOBA_BLOCK, ATTN_WIDTH), lambda i, s, jq, nn: (i, nn[s], COL_AV // ATTN_WIDTH)),
                  pl.BlockSpec((1, LANES, ATTN_WIDTH), lambda i, s, jq, nn: (i, 0, 0)),
                  pl.BlockSpec((ATTN_HEADS, BIAS_ND, MOBA_BLOCK, MOBA_BLOCK), lambda i, s, jq, nn: (0, 0, 0, 0))],
        out_specs=pl.BlockSpec((1, MOBA_BLOCK, ATTN_WIDTH), lambda i, s, jq, nn: (i, jq[s], 0)),
        scratch_shapes=[pltpu.VMEM((ATTN_HEADS, MOBA_BLOCK, LANES), F32),
                        pltpu.VMEM((ATTN_HEADS, MOBA_BLOCK, LANES), F32),
                        pltpu.VMEM((ATTN_HEADS, MOBA_BLOCK, LANES), F32),
                        pltpu.VMEM((ATTN_HEADS, MOBA_BLOCK, LANES), F32),
                        pltpu.VMEM((ATTN_HEADS, MOBA_BLOCK, LANES), F32)])
    return pl.pallas_call(
        _moba_kernel,
        grid_spec=gs,
        out_shape=jax.ShapeDtypeStruct((b, t, ATTN_WIDTH), F32),
        compiler_params=_cparams(("parallel", "arbitrary")),
        name="moba_prompt",
    )(jnp.asarray(jq_tab), jnp.asarray(n_tab), aq, ak, av_p3, kmean, bias_tiles)


def _moba_sample_kernel(pt_ref, q_ref, kn_ref, vn_ref, bias_ref, bown_ref, hm_ref, pick_ref, ck_hbm, cv_hbm,
                        o_ref, kbuf, vbuf, sem, *, layer, npages, page, nseq, nq):
    b = pl.program_id(0)
    slot = lax.rem(b, 2)

    def page_copies(seq, sl):
        out = []
        for p in range(npages):
            pg = pt_ref[seq, p]
            out.append(pltpu.make_async_copy(ck_hbm.at[layer, pg], kbuf.at[sl, pl.ds(p * page, page)], sem.at[0, sl]))
            out.append(pltpu.make_async_copy(cv_hbm.at[layer, pg], vbuf.at[sl, pl.ds(p * page, page)], sem.at[1, sl]))
        return out

    @pl.when(b == 0)
    def _():
        for cp in page_copies(0, 0):
            cp.start()

    @pl.when(b + 1 < nseq)
    def _():
        for cp in page_copies(b + 1, 1 - slot):
            cp.start()

    for cp in page_copies(b, slot):
        cp.wait()

    past = npages * page
    nblk = past // MOBA_BLOCK
    nrep = ATTN_HEADS * nq
    hm = hm_ref[...]
    q = q_ref[0] * (ATTN_HD ** -0.5)
    qrep = _dot_tn(pick_ref[...], q, HI) * hm
    kp = kbuf[slot]
    vp = vbuf[slot]
    kmean = jnp.mean(kp.reshape(nblk, MOBA_BLOCK, ATTN_WIDTH), axis=1)
    sc = _dot_nt(kmean, qrep, HI)
    blk = lax.broadcasted_iota(jnp.int32, (nblk, 1), 0)
    sel = jnp.zeros(sc.shape, F32)
    for _ in range(min(MOBA_TOPK, nblk)):
        mx = jnp.max(sc, axis=0, keepdims=True)
        idx = jnp.min(jnp.where(sc == mx, blk, nblk), axis=0, keepdims=True)
        pick = blk == idx
        sel = jnp.where(pick, 1.0, sel)
        sc = jnp.where(pick, LOWEST, sc)
    selk = jnp.broadcast_to(sel[:, None, :], (nblk, MOBA_BLOCK, nrep)).reshape(past, nrep) > 0.0

    qb = qrep.astype(BF16)
    lg = jnp.where(selk, _dot_nt(kp.astype(BF16), qb) + bias_ref[...], NEG)
    lo = _dot_nt(kn_ref[0].astype(BF16), qb) + bown_ref[...]
    m = jnp.maximum(jnp.max(lg, axis=0, keepdims=True), jnp.max(lo, axis=0, keepdims=True))
    pp = jnp.where(selk, jnp.exp(lg - m), 0.0)
    po = jnp.exp(lo - m)
    ones = jnp.ones((past, LANES), F32)
    den = _dot_tn(pp, ones, HI) + _dot_tn(po, ones[0:SUBLANES], HI)
    r = _dot_tn(pp.astype(BF16), vp.astype(BF16)) + _dot_tn(po.astype(BF16), vn_ref[0].astype(BF16))
    r = r / den[:, 0:1] * hm
    o_ref[0] = _dot(pick_ref[...], r, HI).astype(o_ref.dtype)


def _moba_sample(page_table, aq, ak, av_p3, bias_t, bias_own, head_mask, pick, cache_k, cache_v, layer, nq):
    nseq = aq.shape[0]
    npages = page_table.shape[1]
    page = cache_k.shape[2]
    past = npages * page
    nrep = ATTN_HEADS * nq
    kern = functools.partial(_moba_sample_kernel, layer=layer, npages=npages, page=page, nseq=nseq, nq=nq)
    gs = pltpu.PrefetchScalarGridSpec(
        num_scalar_prefetch=1,
        grid=(nseq,),
        in_specs=[pl.BlockSpec((1, SUBLANES, ATTN_WIDTH), lambda i, pt: (i, 0, 0)),
                  pl.BlockSpec((1, SUBLANES, ATTN_WIDTH), lambda i, pt: (i, 0, 0)),
                  pl.BlockSpec((1, SUBLANES, ATTN_WIDTH), lambda i, pt: (i, 0, COL_AV // ATTN_WIDTH)),
                  pl.BlockSpec((past, nrep), lambda i, pt: (0, 0)),
                  pl.BlockSpec((SUBLANES, nrep), lambda i, pt: (0, 0)),
                  pl.BlockSpec((nrep, ATTN_WIDTH), lambda i, pt: (0, 0)),
                  pl.BlockSpec((SUBLANES, nrep), lambda i, pt: (0, 0)),
                  pl.BlockSpec(memory_space=pl.ANY),
                  pl.BlockSpec(memory_space=pl.ANY)],
        out_specs=pl.BlockSpec((1, SUBLANES, ATTN_WIDTH), lambda i, pt: (i, 0, 0)),
        scratch_shapes=[pltpu.VMEM((2, past, ATTN_WIDTH), F32),
                        pltpu.VMEM((2, past, ATTN_WIDTH), F32),
                        pltpu.SemaphoreType.DMA((2, 2))])
    return pl.pallas_call(
        kern,
        grid_spec=gs,
        out_shape=jax.ShapeDtypeStruct((nseq, SUBLANES, ATTN_WIDTH), F32),
        compiler_params=_cparams(("arbitrary",)),
        name="moba_sample",
    )(page_table, aq, ak, av_p3, bias_t, bias_own, head_mask, pick, cache_k, cache_v)


def _merge_kernel(x_ref, og_ref, oa_ref, os_ref, g0_ref, g1_ref, g2_ref, wg_ref, wa_ref, ws_ref, wo_ref, o_ref):
    merged = (_sigmoid(g0_ref[...]) * _dot(og_ref[...].astype(BF16), wg_ref[...])
              + _sigmoid(g1_ref[...]) * _dot(oa_ref[...].astype(BF16), wa_ref[...])
              + _sigmoid(g2_ref[...]) * _dot(os_ref[...].astype(BF16), ws_ref[...]))
    o_ref[...] = x_ref[...] + _dot(merged.astype(BF16), wo_ref[...])


def _merge(x, o_gdn, o_att, o_ssm, p2, wg, wa, ws, wo, tm):
    m = x.shape[0]
    row = lambda w, cb=0: pl.BlockSpec((tm, w), lambda i: (i, cb))
    fixed = lambda a, bb: pl.BlockSpec((a, bb), lambda i: (0, 0))
    return pl.pallas_call(
        _merge_kernel,
        grid=(m // tm,),
        in_specs=[row(D_MODEL), row(GDN_VAL), row(ATTN_WIDTH), row(SSM_INNER),
                  row(D_MODEL, COL_G0 // D_MODEL), row(D_MODEL, COL_G1 // D_MODEL), row(D_MODEL, COL_G2 // D_MODEL),
                  fixed(GDN_VAL, D_MODEL), fixed(ATTN_WIDTH, D_MODEL), fixed(SSM_INNER, D_MODEL),
                  fixed(D_MODEL, D_MODEL)],
        out_specs=row(D_MODEL),
        out_shape=jax.ShapeDtypeStruct((m, D_MODEL), F32),
        compiler_params=_cparams(("parallel",)),
        name="merge",
    )(x, o_gdn, o_att, o_ssm, p2, p2, p2, wg, wa, ws, wo)


FFN_HALO = 16


def _ffn_prompt_kernel(x_ref, xh_ref, nw_ref, upg_ref, upv_ref, cwg_ref, cwv_ref, cbg_ref, cbv_ref, dn_ref,
                       o_ref, tg_ref, tv_ref, h_ref, ug_ref, uv_ref, *, tm, seq_tiles):
    i = pl.program_id(0)
    j = pl.program_id(1)

    @pl.when(j == 0)
    def _():
        x = x_ref[...]
        h_ref[FFN_HALO:, :] = _rms(x, nw_ref[...]).astype(BF16)
        h_ref[0:FFN_HALO, :] = _rms(xh_ref[...], nw_ref[...]).astype(BF16)
        o_ref[...] = x

    first = lax.rem(i, seq_tiles) == 0
    rows = lax.broadcasted_iota(jnp.int32, (tm + FFN_HALO, 1), 0)
    keep = jnp.where(first & (rows < FFN_HALO), 0.0, 1.0)
    h = h_ref[...]
    ug_ref[...] = _dot(h, upg_ref[...]) * keep
    uv_ref[...] = _dot(h, upv_ref[...]) * keep

    def conv(u_ref, cw_ref, cb_ref):
        acc = cb_ref[...]
        for k in range(FFN_CONV):
            off = FFN_HALO - (FFN_CONV - 1) + k
            acc = acc + cw_ref[k:k + 1, :] * u_ref[off:off + tm, :]
        return acc

    act = _silu(conv(ug_ref, cwg_ref, cbg_ref)) * conv(uv_ref, cwv_ref, cbv_ref)
    o_ref[...] += _dot(act.astype(BF16), dn_ref[...])
    tg_ref[0] = ug_ref[tm + FFN_HALO - SUBLANES:, :]
    tv_ref[0] = uv_ref[tm + FFN_HALO - SUBLANES:, :]


def _ffn_prompt(x, nw, up, cw, cb, dn, tm, seq_len):
    m = x.shape[0]
    nt = m // tm
    nj = D_FF // FFN_TN
    hb = tm // FFN_HALO
    kern = functools.partial(_ffn_prompt_kernel, tm=tm, seq_tiles=seq_len // tm)
    vec = lambda r, off: pl.BlockSpec((r, FFN_TN), lambda i, j: (0, j + off))
    return pl.pallas_call(
        kern,
        grid=(nt, nj),
        in_specs=[pl.BlockSpec((tm, D_MODEL), lambda i, j: (i, 0)),
                  pl.BlockSpec((FFN_HALO, D_MODEL), lambda i, j: (jnp.maximum(i * hb - 1, 0), 0)),
                  pl.BlockSpec((1, D_MODEL), lambda i, j: (0, 0)),
                  pl.BlockSpec((D_MODEL, FFN_TN), lambda i, j: (0, j)),
                  pl.BlockSpec((D_MODEL, FFN_TN), lambda i, j: (0, j + nj)),
                  vec(FFN_CONV, 0), vec(FFN_CONV, nj), vec(1, 0), vec(1, nj),
                  pl.BlockSpec((FFN_TN, D_MODEL), lambda i, j: (j, 0))],
        out_specs=[pl.BlockSpec((tm, D_MODEL), lambda i, j: (i, 0)),
                   pl.BlockSpec((1, SUBLANES, FFN_TN), lambda i, j: (i, 0, j)),
                   pl.BlockSpec((1, SUBLANES, FFN_TN), lambda i, j: (i, 0, j))],
        out_shape=[jax.ShapeDtypeStruct((m, D_MODEL), F32),
                   jax.ShapeDtypeStruct((nt, SUBLANES, D_FF), F32),
                   jax.ShapeDtypeStruct((nt, SUBLANES, D_FF), F32)],
        scratch_shapes=[pltpu.VMEM((tm + FFN_HALO, D_MODEL), BF16),
                        pltpu.VMEM((tm + FFN_HALO, FFN_TN), F32),
                        pltpu.VMEM((tm + FFN_HALO, FFN_TN), F32)],
        compiler_params=_cparams(("parallel", "arbitrary")),
        name="ffn_prompt",
    )(x, x, nw, up, up, cw, cw, cb, cb, dn)


def _ffn_sample_kernel(x_ref, nw_ref, upg_ref, upv_ref, cwg_ref, cwv_ref, cbg_ref, cbv_ref, dn_ref, sg_ref, sv_ref,
                       o_ref, ng_ref, nv_ref, h_ref, *, nb):
    j = pl.program_id(0)

    @pl.when(j == 0)
    def _():
        x = x_ref[...]
        h_ref[...] = _rms(x, nw_ref[...]).astype(BF16)
        o_ref[...] = x

    h = h_ref[...]
    hist = FFN_CONV - 1

    def conv(u, s_ref, cw_ref, cb_ref, n_ref):
        full = jnp.concatenate([s_ref[...], u], axis=0)
        rows = u.shape[0]
        acc = cb_ref[...]
        for k in range(FFN_CONV):
            acc = acc + cw_ref[k:k + 1, :] * full[k * nb:k * nb + rows, :]
        n_ref[...] = full[rows:, :]
        return acc

    cg = conv(_dot(h, upg_ref[...]), sg_ref, cwg_ref, cbg_ref, ng_ref)
    cv = conv(_dot(h, upv_ref[...]), sv_ref, cwv_ref, cbv_ref, nv_ref)
    o_ref[...] += _dot((_silu(cg) * cv).astype(BF16), dn_ref[...])


def _ffn_sample(x, nw, up, cw, cb, dn, state_g, state_v, nb):
    m = x.shape[0]
    nj = D_FF // FFN_TN
    srows = (FFN_CONV - 1) * nb
    kern = functools.partial(_ffn_sample_kernel, nb=nb)
    vec = lambda r, off: pl.BlockSpec((r, FFN_TN), lambda j: (0, j + off))
    return pl.pallas_call(
        kern,
        grid=(nj,),
        in_specs=[pl.BlockSpec((m, D_MODEL), lambda j: (0, 0)),
                  pl.BlockSpec((1, D_MODEL), lambda j: (0, 0)),
                  pl.BlockSpec((D_MODEL, FFN_TN), lambda j: (0, j)),
                  pl.BlockSpec((D_MODEL, FFN_TN), lambda j: (0, j + nj)),
                  vec(FFN_CONV, 0), vec(FFN_CONV, nj), vec(1, 0), vec(1, nj),
                  pl.BlockSpec((FFN_TN, D_MODEL), lambda j: (j, 0)),
                  pl.BlockSpec((srows, FFN_TN), lambda j: (0, j)),
                  pl.BlockSpec((srows, FFN_TN), lambda j: (0, j))],
        out_specs=[pl.BlockSpec((m, D_MODEL), lambda j: (0, 0)),
                   pl.BlockSpec((srows, FFN_TN), lambda j: (0, j)),
                   pl.BlockSpec((srows, FFN_TN), lambda j: (0, j))],
        out_shape=[jax.ShapeDtypeStruct((m, D_MODEL), F32),
                   jax.ShapeDtypeStruct((srows, D_FF), F32),
                   jax.ShapeDtypeStruct((srows, D_FF), F32)],
        scratch_shapes=[pltpu.VMEM((m, D_MODEL), BF16)],
        compiler_params=_cparams(("arbitrary",)),
        name="ffn_sample",
    )(x, nw, up, up, cw, cw, cb, cb, dn, state_g, state_v)


def _t5_bucket(rel):
    n = jnp.maximum(rel, 0)
    exact = NUM_BUCKETS // 2
    nf = jnp.maximum(n, 1).astype(F32)
    large = exact + (jnp.log(nf / exact) / math.log(MAX_DISTANCE / exact) * (NUM_BUCKETS - exact)).astype(jnp.int32)
    return jnp.where(n < exact, n, jnp.minimum(large, NUM_BUCKETS - 1))


def _bias_by_distance(rel_bias, max_rel):
    return rel_bias.astype(F32)[_t5_bucket(jnp.arange(max_rel)), :].T


def _prompt_bias_tiles(rel_bias):
    tab = _bias_by_distance(rel_bias, (BIAS_ND + 1) * MOBA_BLOCK)
    qi = np.arange(MOBA_BLOCK)[:, None]
    ki = np.arange(MOBA_BLOCK)[None, :]
    tiles = []
    for d in range(BIAS_ND):
        rel = np.maximum(d * MOBA_BLOCK + qi - ki, 0)
        tiles.append(tab[:, rel])
    return jnp.stack(tiles, axis=1)


def _sample_bias_tables(rel_bias, past_len, nq):
    tab = _bias_by_distance(rel_bias, past_len + nq)
    kpos = np.arange(past_len)[:, None]
    rel = past_len + np.arange(nq)[None, :] - kpos
    past_t = jnp.transpose(tab[:, rel], (1, 0, 2)).reshape(past_len, ATTN_HEADS * nq)
    ko = np.arange(SUBLANES)[:, None]
    relo = np.arange(nq)[None, :] - ko
    own = jnp.transpose(tab[:, np.maximum(relo, 0)], (1, 0, 2))
    ok = jnp.asarray(((relo >= 0) & (ko < nq))[:, None, :])
    own_t = jnp.where(ok, own, NEG).reshape(SUBLANES, ATTN_HEADS * nq)
    return past_t, own_t


def _permute_w_in(w):
    cuts = np.cumsum((0,) + IN_SIZES)
    sec = lambda k, lo=0, hi=None: w[:, cuts[k] + lo: (cuts[k + 1] if hi is None else cuts[k] + hi)]
    nbc = SSM_GROUPS * SSM_DSTATE
    small = jnp.concatenate([sec(2), sec(3), sec(9),
                             jnp.zeros((w.shape[0], LANES - 2 * GDN_HEADS - SSM_HEADS), w.dtype)], axis=1)
    parts = [sec(7), sec(8, 0, SSM_INNER),
             sec(10, 0, D_MODEL), sec(10, D_MODEL, 2 * D_MODEL), sec(10, 2 * D_MODEL, 3 * D_MODEL),
             sec(0, 0, GDN_KEY), sec(0, GDN_KEY, 2 * GDN_KEY), sec(0, 2 * GDN_KEY, GDN_CONV_DIM), sec(1),
             sec(4), sec(5), sec(6),
             sec(8, SSM_INNER, SSM_INNER + nbc), sec(8, SSM_INNER + nbc, SSM_CONV_DIM), small]
    return jnp.concatenate(parts, axis=1).astype(BF16)


def _lane_row(vals, offset):
    return jnp.zeros((LANES,), F32).at[offset:offset + vals.shape[0]].set(vals.astype(F32))


def _pick_tm(m, cap):
    tm = cap
    while m % tm:
        tm //= 2
    return tm


def kernel(x_prompt, x_sample, cache_k, cache_v, page_table, state_gdn, state_gdn_conv, state_ssm, state_ssm_conv, state_ffn_conv, rel_bias, norm_mix, w_in, gdn_conv_w, gdn_A_log, gdn_dt_bias, gdn_norm, q_norm, k_norm, ssm_conv_w, ssm_conv_b, ssm_A_log, ssm_dt_bias, ssm_D, ssm_norm, w_br_gdn, w_br_attn, w_br_ssm, w_out, norm_ffn, ffn_up, ffn_conv_w, ffn_conv_b, ffn_down):
    depth = w_in.shape[0]
    bp, seq, _ = x_prompt.shape
    bd, nq, _ = x_sample.shape
    npool, page = cache_k.shape[1], cache_k.shape[2]
    npages = page_table.shape[1]
    past_len = npages * page
    nblk = seq // MOBA_BLOCK
    assert seq % MOBA_BLOCK == 0 and nblk <= LANES and past_len % MOBA_BLOCK == 0
    assert nq <= SUBLANES and bd % SUBLANES == 0

    ck = cache_k.reshape(depth, npool, page, ATTN_WIDTH)
    cv = cache_v.reshape(depth, npool, page, ATTN_WIDTH)
    bias_tiles = _prompt_bias_tiles(rel_bias)
    bias_past, bias_own = _sample_bias_tables(rel_bias, past_len, nq)
    nrep = ATTN_HEADS * nq
    col_head = np.arange(nrep) // nq
    head_mask = jnp.asarray((np.arange(ATTN_WIDTH)[None, :] // ATTN_HD == col_head[:, None]).astype(np.float32))
    pick = jnp.asarray((np.arange(SUBLANES)[:, None] == (np.arange(nrep) % nq)[None, :]).astype(np.float32))
    ones_bd = jnp.asarray((np.arange(ATTN_WIDTH)[:, None] // ATTN_HD
                           == np.arange(ATTN_WIDTH)[None, :] // ATTN_HD).astype(np.float32)).astype(BF16)
    expand = jnp.asarray((np.arange(LANES)[:, None] - SM_DT
                          == np.arange(SSM_INNER)[None, :] // SSM_HEADDIM).astype(np.float32))
    zeros_gdn_state = jnp.zeros((1, bp, GDN_HEADS, GDN_DK, GDN_DV), F32)
    zeros_gdn_conv = jnp.zeros((1, bp, GDN_CONV - 1, GDN_CONV_DIM), F32)
    zeros_ssm_state = jnp.zeros((1, bp, SSM_HEADS, SSM_HEADDIM, SSM_DSTATE), F32)
    zeros_ssm_conv = jnp.zeros((1, bp, SSM_CONV - 1, SSM_CONV_DIM), F32)

    mp = bp * seq
    tm_p = _pick_tm(mp, 1024)
    tm_ffn = _pick_tm(seq, 512)
    chunk = math.gcd(seq, 64)
    ms = bd * SUBLANES
    yp = x_prompt.reshape(mp, D_MODEL)
    ys = x_sample
    out_p = [[] for _ in range(7)]
    out_s = [[] for _ in range(7)]

    for l in range(depth):
        w_perm = _permute_w_in(w_in[l])
        nmix = norm_mix[l].reshape(1, D_MODEL)
        gdn_par = jnp.zeros((SUBLANES, LANES), F32).at[0].set(_lane_row(gdn_dt_bias[l], SM_A)).at[1].set(
            _lane_row(gdn_A_log[l], SM_A))
        ssm_par = jnp.zeros((SUBLANES, LANES), F32).at[0].set(_lane_row(ssm_dt_bias[l], SM_DT)).at[1].set(
            _lane_row(ssm_A_log[l], SM_DT))
        gnw = gdn_norm[l].reshape(1, GDN_DV)
        qw = jnp.tile(q_norm[l], ATTN_HEADS).reshape(1, ATTN_WIDTH)
        kw = jnp.tile(k_norm[l], ATTN_HEADS).reshape(1, ATTN_WIDTH)
        scb = ssm_conv_b[l].reshape(1, SSM_CONV_DIM)
        dexp = jnp.repeat(ssm_D[l], SSM_HEADDIM).reshape(1, SSM_INNER)
        snw = ssm_norm[l].reshape(1, SSM_INNER)
        wg, wa, ws, wo = (w_br_gdn[l].astype(BF16), w_br_attn[l].astype(BF16), w_br_ssm[l].astype(BF16),
                          w_out[l].astype(BF16))
        nffn = norm_ffn[l].reshape(1, D_MODEL)
        up = ffn_up[l].astype(BF16)
        dn = ffn_down[l].astype(BF16)
        fcw = ffn_conv_w[l]
        fcb = ffn_conv_b[l].reshape(1, 2 * D_FF)

        p2 = _norm_proj(yp, nmix, w_perm, tm_p)
        p3 = p2.reshape(bp, seq, P_WIDTH)
        og, gs_new, gc_new = _gdn(p3, gdn_conv_w[l], gdn_par, gnw, zeros_gdn_conv, zeros_gdn_state, 0,
                                  G=bp, C=chunk, L=chunk)
        osm, hs_new, sc_new = _ssd(p3, ssm_conv_w[l], scb, ssm_par, dexp, snw, expand, zeros_ssm_conv,
                                   zeros_ssm_state, 0, G=bp, C=chunk, L=chunk)
        aq, ak, km = _qknorm(p2, qw, kw, ones_bd, MOBA_BLOCK)
        kmean = jnp.pad(km.reshape(bp, nblk, ATTN_WIDTH), ((0, 0), (0, LANES - nblk), (0, 0)))
        oa = _moba_prompt(aq.reshape(bp, seq, ATTN_WIDTH), ak.reshape(bp, seq, ATTN_WIDTH), p3, kmean, bias_tiles)
        x1 = _merge(yp, og.reshape(mp, GDN_VAL), oa.reshape(mp, ATTN_WIDTH), osm.reshape(mp, SSM_INNER), p2,
                    wg, wa, ws, wo, _pick_tm(mp, 512))
        yp, tg, tv = _ffn_prompt(x1, nffn, up, fcw, fcb, dn, tm_ffn, seq)
        tpb = seq // tm_ffn
        last = np.arange(bp) * tpb + tpb - 1
        ffn_state = jnp.concatenate([tg[last], tv[last]], axis=-1)[:, SUBLANES - (FFN_CONV - 1):, :]
        for acc, tns in zip(out_p, (ak.reshape(bp, seq, ATTN_HEADS, ATTN_HD),
                                    p3[:, :, COL_AV:COL_AV + ATTN_WIDTH].reshape(bp, seq, ATTN_HEADS, ATTN_HD),
                                    gs_new, gc_new, hs_new, sc_new, ffn_state)):
            acc.append(tns)

        xs_pad = jnp.pad(ys, ((0, 0), (0, SUBLANES - nq), (0, 0))).reshape(ms, D_MODEL)
        q2 = _norm_proj(xs_pad, nmix, w_perm, _pick_tm(ms, 1024))
        q3 = q2.reshape(bd, SUBLANES, P_WIDTH)
        gsz = 2
        og_s, gs_s, gc_s = _gdn(q3, gdn_conv_w[l], gdn_par, gnw, state_gdn_conv, state_gdn, l,
                                G=gsz, C=SUBLANES, L=nq)
        os_s, hs_s, sc_s = _ssd(q3, ssm_conv_w[l], scb, ssm_par, dexp, snw, expand, state_ssm_conv, state_ssm, l,
                                G=gsz, C=SUBLANES, L=nq)
        aq_s, ak_s, _ = _qknorm(q2, qw, kw, ones_bd, _pick_tm(ms, MOBA_BLOCK))
        oa_s = _moba_sample(page_table, aq_s.reshape(bd, SUBLANES, ATTN_WIDTH), ak_s.reshape(bd, SUBLANES, ATTN_WIDTH),
                            q3, bias_past, bias_own, head_mask, pick, ck, cv, l, nq)
        x1_s = _merge(xs_pad, og_s.reshape(ms, GDN_VAL), oa_s.reshape(ms, ATTN_WIDTH), os_s.reshape(ms, SSM_INNER), q2,
                      wg, wa, ws, wo, _pick_tm(ms, 512))
        x1_t = jnp.transpose(x1_s.reshape(bd, SUBLANES, D_MODEL)[:, :nq], (1, 0, 2)).reshape(nq * bd, D_MODEL)
        st = jnp.transpose(state_ffn_conv[l], (1, 0, 2)).reshape((FFN_CONV - 1) * bd, 2 * D_FF)
        y_t, ng, nv = _ffn_sample(x1_t, nffn, up, fcw, fcb, dn, st[:, :D_FF], st[:, D_FF:], bd)
        ys = jnp.transpose(y_t.reshape(nq, bd, D_MODEL), (1, 0, 2))
        ffn_state_s = jnp.transpose(jnp.concatenate([ng, nv], axis=-1).reshape(FFN_CONV - 1, bd, 2 * D_FF), (1, 0, 2))
        for acc, tns in zip(out_s, (ak_s.reshape(bd, SUBLANES, ATTN_HEADS, ATTN_HD)[:, :nq],
                                    q3[:, :nq, COL_AV:COL_AV + ATTN_WIDTH].reshape(bd, nq, ATTN_HEADS, ATTN_HD),
                                    gs_s, gc_s, hs_s, sc_s, ffn_state_s)):
            acc.append(tns)

    k_p, v_p, gdn_p, gdn_conv_p, ssm_p, ssm_conv_p, ffn_conv_p = [jnp.stack(a) for a in out_p]
    k_s, v_s, gdn_s, gdn_conv_s, ssm_s, ssm_conv_s, ffn_conv_s = [jnp.stack(a) for a in out_s]
    return (yp.reshape(bp, seq, D_MODEL), ys, k_p, v_p, gdn_p, gdn_conv_p, ssm_p, ssm_conv_p, ffn_conv_p,
            k_s, v_s, gdn_s, gdn_conv_s, ssm_s, ssm_conv_s, ffn_conv_s)
```

```python
import functools
import math

import numpy as np
import jax
import jax.numpy as jnp
from jax import lax
from jax.experimental import pallas as pl
from jax.experimental.pallas import tpu as pltpu

F32 = jnp.float32
BF16 = jnp.bfloat16
HI = lax.Precision.HIGHEST

D_MODEL = 1024
GDN_HEADS, GDN_DK, GDN_DV, GDN_CONV = 4, 128, 128, 4
GDN_KEY = GDN_HEADS * GDN_DK
GDN_VAL = GDN_HEADS * GDN_DV
GDN_CONV_DIM = 2 * GDN_KEY + GDN_VAL
ATTN_HEADS, ATTN_HD = 8, 64
ATTN_WIDTH = ATTN_HEADS * ATTN_HD
MOBA_BLOCK, MOBA_TOPK = 256, 3
NUM_BUCKETS, MAX_DISTANCE = 32, 1024
SSM_INNER, SSM_HEADDIM, SSM_GROUPS, SSM_DSTATE, SSM_CONV = 1024, 64, 4, 128, 4
SSM_HEADS = SSM_INNER // SSM_HEADDIM
SSM_CONV_DIM = SSM_INNER + 2 * SSM_GROUPS * SSM_DSTATE
N_BRANCH = 3
D_FF = 11 * D_MODEL // 4
FFN_CONV = 3
IN_SIZES = (GDN_CONV_DIM, GDN_VAL, GDN_HEADS, GDN_HEADS, ATTN_WIDTH, ATTN_WIDTH, ATTN_WIDTH,
            SSM_INNER, SSM_CONV_DIM, SSM_HEADS, N_BRANCH * D_MODEL)
EPS = 1e-6
NEG = -1e30
LOWEST = -3e38

LANES = 128
SUBLANES = 8
VMEM_LIMIT = 56 * 1024 * 1024

COL_SZ, COL_SX, COL_G0, COL_G1, COL_G2 = 0, 1024, 2048, 3072, 4096
COL_GQ, COL_GK, COL_GV, COL_GZ = 5120, 5632, 6144, 6656
COL_AQ, COL_AK, COL_AV = 7168, 7680, 8192
COL_SB, COL_SC = 8704, 9216
COL_SMALL = 9728
P_WIDTH = COL_SMALL + LANES
SM_A, SM_B, SM_DT = 0, GDN_HEADS, 2 * GDN_HEADS
FFN_TN = 1408
BIAS_ND = 6


def _cparams(sem):
    return pltpu.CompilerParams(dimension_semantics=sem, vmem_limit_bytes=VMEM_LIMIT)


def _dot(a, b, prec=None):
    return lax.dot_general(a, b, (((1,), (0,)), ((), ())), precision=prec, preferred_element_type=F32)


def _dot_nt(a, b, prec=None):
    return lax.dot_general(a, b, (((1,), (1,)), ((), ())), precision=prec, preferred_element_type=F32)


def _dot_tn(a, b, prec=None):
    return lax.dot_general(a, b, (((0,), (0,)), ((), ())), precision=prec, preferred_element_type=F32)


def _sigmoid(x):
    return 1.0 / (1.0 + jnp.exp(-x))


def _silu(x):
    return x * _sigmoid(x)


def _softplus(x):
    return jnp.maximum(x, 0.0) + jnp.log1p(jnp.exp(-jnp.abs(x)))


def _rms(x, w):
    return x * lax.rsqrt(jnp.mean(x * x, axis=-1, keepdims=True) + EPS) * w


def _norm_proj_kernel(x_ref, nw_ref, w_ref, o_ref, h_ref):
    @pl.when(pl.program_id(1) == 0)
    def _():
        h_ref[...] = _rms(x_ref[...], nw_ref[...]).astype(BF16)

    o_ref[...] = _dot(h_ref[...], w_ref[...])


def _norm_proj(x, nw, w, tm):
    m = x.shape[0]
    tn = 1408
    return pl.pallas_call(
        _norm_proj_kernel,
        grid=(m // tm, P_WIDTH // tn),
        in_specs=[pl.BlockSpec((tm, D_MODEL), lambda i, j: (i, 0)),
                  pl.BlockSpec((1, D_MODEL), lambda i, j: (0, 0)),
                  pl.BlockSpec((D_MODEL, tn), lambda i, j: (0, j))],
        out_specs=pl.BlockSpec((tm, tn), lambda i, j: (i, j)),
        out_shape=jax.ShapeDtypeStruct((m, P_WIDTH), F32),
        scratch_shapes=[pltpu.VMEM((tm, D_MODEL), BF16)],
        compiler_params=_cparams(("parallel", "arbitrary")),
        name="norm_proj",
    )(x, nw, w)


def _conv_from_ext(ext_ref, g, cw, c, width):
    acc = None
    for k in range(width):
        off = SUBLANES - (width - 1) + k
        term = cw[k:k + 1, :] * ext_ref[g, off:off + c, :]
        acc = term if acc is None else acc + term
    return acc


def _tri_masks(c):
    r = lax.broadcasted_iota(jnp.int32, (c, c), 0)
    s = lax.broadcasted_iota(jnp.int32, (c, c), 1)
    return r >= s, r > s


def _split2(x):
    hi = x.astype(BF16)
    return hi, (x - hi.astype(F32)).astype(BF16)


def _split3(x):
    a = x.astype(BF16)
    r = x - a.astype(F32)
    b = r.astype(BF16)
    return a, b, (r - b.astype(F32)).astype(BF16)


def _dot3(a2, b2, dot=None):
    dot = _dot if dot is None else dot
    return dot(a2[0], b2[0]) + (dot(a2[0], b2[1]) + dot(a2[1], b2[0]))


def _mask_dot(mask, x3):
    return _dot(mask, x3[0]) + (_dot(mask, x3[1]) + _dot(mask, x3[2]))


def _mask_dot_rhs(x3, mask):
    return _dot(x3[0], mask) + (_dot(x3[1], mask) + _dot(x3[2], mask))


def _mask_dot_tn(x3, mask):
    return _dot_tn(x3[0], mask) + (_dot_tn(x3[1], mask) + _dot_tn(x3[2], mask))


def _unit_lower_inverses(n_list, c):
    r = lax.broadcasted_iota(jnp.int32, (c, c), 0)
    s = lax.broadcasted_iota(jnp.int32, (c, c), 1)
    eye = jnp.where(r == s, 1.0, 0.0).astype(F32)
    ts = [eye - n for n in n_list]
    p2 = [_split2(-n) for n in n_list]
    span = 2
    while span < c:
        p2 = [_split2(_dot3(q, q)) for q in p2]
        ts = [t + _dot3(_split2(t), q) for t, q in zip(ts, p2)]
        span *= 2
    return ts


def _gdn_kernel(q_ref, k_ref, v_ref, z_ref, sm_ref, cw_ref, par_ref, nw_ref, cs_ref, s0_ref,
                o_ref, so_ref, co_ref, ext_ref, s_ref, *, G, C, L, NC):
    ci = pl.program_id(1)
    hist = GDN_CONV - 1

    @pl.when(ci == 0)
    def _():
        s_ref[...] = s0_ref[...]
        ext_ref[...] = jnp.zeros(ext_ref.shape, F32)
        for g in range(G):
            ext_ref[g, SUBLANES - hist:SUBLANES, :] = cs_ref[g]

    causal, strict = _tri_masks(C)
    tril = jnp.where(causal, 1.0, 0.0).astype(BF16)
    triu = jnp.where(lax.broadcasted_iota(jnp.int32, (C, C), 0)
                     <= lax.broadcasted_iota(jnp.int32, (C, C), 1), 1.0, 0.0).astype(BF16)
    rows = lax.broadcasted_iota(jnp.int32, (C, 1), 0)
    valid = rows < L
    lane = lax.broadcasted_iota(jnp.int32, (1, LANES), 1)
    dt_bias = par_ref[0:1, :]
    neg_a = -jnp.exp(par_ref[1:2, :])
    cw = cw_ref[...]
    nw = nw_ref[...]

    hs = []
    for g in range(G):
        ext_ref[g, SUBLANES:SUBLANES + C, 0:GDN_KEY] = q_ref[g]
        ext_ref[g, SUBLANES:SUBLANES + C, GDN_KEY:2 * GDN_KEY] = k_ref[g]
        ext_ref[g, SUBLANES:SUBLANES + C, 2 * GDN_KEY:] = v_ref[g]
        xc = _silu(_conv_from_ext(ext_ref, g, cw, C, GDN_CONV))
        new_hist = ext_ref[g, SUBLANES + L - hist:SUBLANES + L, :]
        co_ref[g] = new_hist
        ext_ref[g, SUBLANES - hist:SUBLANES, :] = new_hist

        small = sm_ref[g]
        gmat = jnp.where(valid & (lane < SM_B), neg_a * _softplus(small + dt_bias), 0.0)
        bmat = jnp.where(valid, _sigmoid(small), 0.0)
        g3 = _split3(gmat)
        gc = _mask_dot(tril, g3)
        gct = _mask_dot_tn(g3, triu)
        z = z_ref[g]
        for h in range(GDN_HEADS):
            gcol = gc[:, SM_A + h:SM_A + h + 1]
            grow = gct[SM_A + h:SM_A + h + 1, :]
            glast = gc[C - 1:C, SM_A + h:SM_A + h + 1]
            bcol = bmat[:, SM_B + h:SM_B + h + 1]
            qh = xc[:, h * GDN_DK:(h + 1) * GDN_DK]
            kh = xc[:, GDN_KEY + h * GDN_DK:GDN_KEY + (h + 1) * GDN_DK]
            vh = xc[:, 2 * GDN_KEY + h * GDN_DV:2 * GDN_KEY + (h + 1) * GDN_DV]
            qh = qh * lax.rsqrt(jnp.sum(qh * qh, axis=-1, keepdims=True) + EPS) * (GDN_DK ** -0.5)
            kh = jnp.where(valid, kh * lax.rsqrt(jnp.sum(kh * kh, axis=-1, keepdims=True) + EPS), 0.0)
            vh = jnp.where(valid, vh, 0.0)
            ecol = jnp.exp(gcol)
            hs.append(dict(
                g=g, h=h, bcol=bcol, q=qh, k=kh, kb=kh.astype(BF16), glast=glast,
                gam=jnp.where(causal, jnp.exp(jnp.minimum(gcol - grow, 0.0)), 0.0),
                rhs=jnp.concatenate([vh * bcol, kh * (bcol * ecol)], axis=1),
                q_dec=(qh * ecol).astype(BF16), k_dec=(kh * jnp.exp(glast - gcol)).astype(BF16),
                z=z[:, h * GDN_DV:(h + 1) * GDN_DV], s=s_ref[g, h]))

    kks = [_dot_nt(d["kb"], d["kb"]) for d in hs]
    tinvs = _unit_lower_inverses([jnp.where(strict, d["bcol"] * kk * d["gam"], 0.0) for d, kk in zip(hs, kks)], C)
    us = [_dot3(_split2(t), _split2(d["rhs"])) for d, t in zip(hs, tinvs)]
    qks = [(_dot_nt(d["q"].astype(BF16), d["kb"]) * d["gam"]).astype(BF16) for d in hs]
    sbs = [d["s"].astype(BF16) for d in hs]
    ws = [u[:, :GDN_DV] - _dot(u[:, GDN_DV:].astype(BF16), sb) for u, sb in zip(us, sbs)]
    wbs = [w.astype(BF16) for w in ws]
    outs = [_dot(d["q_dec"], sb) + _dot(qk, wb) for d, sb, qk, wb in zip(hs, sbs, qks, wbs)]
    s_new = [d["s"] * jnp.exp(d["glast"]) + _dot_tn(d["k_dec"], wb) for d, wb in zip(hs, wbs)]
    outs = [_rms(o, nw) * _silu(d["z"]) for d, o in zip(hs, outs)]
    o_ref[...] = jnp.stack([jnp.concatenate(outs[g * GDN_HEADS:(g + 1) * GDN_HEADS], axis=1)
                            for g in range(G)]).astype(o_ref.dtype)
    s_ref[...] = jnp.stack(s_new).reshape(s_ref.shape)

    @pl.when(ci == NC - 1)
    def _():
        so_ref[...] = s_ref[...]


def _gdn(p3, conv_w, par, nw, conv_state, state, layer, *, G, C, L):
    b, t, _ = p3.shape
    nc = t // C
    blk = lambda col, w: pl.BlockSpec((G, C, w), lambda i, c: (i, c, col // w))
    fixed = lambda shape: pl.BlockSpec(shape, lambda i, c: (0,) * len(shape))
    kern = functools.partial(_gdn_kernel, G=G, C=C, L=L, NC=nc)
    return pl.pallas_call(
        kern,
        grid=(b // G, nc),
        in_specs=[blk(COL_GQ, GDN_KEY), blk(COL_GK, GDN_KEY), blk(COL_GV, GDN_VAL), blk(COL_GZ, GDN_VAL),
                  blk(COL_SMALL, LANES),
                  fixed((GDN_CONV, GDN_CONV_DIM)), fixed((SUBLANES, LANES)), fixed((1, GDN_DV)),
                  pl.BlockSpec((None, G, GDN_CONV - 1, GDN_CONV_DIM), lambda i, c: (layer, i, 0, 0)),
                  pl.BlockSpec((None, G, GDN_HEADS, GDN_DK, GDN_DV), lambda i, c: (layer, i, 0, 0, 0))],
        out_specs=[pl.BlockSpec((G, C, GDN_VAL), lambda i, c: (i, c, 0)),
                   pl.BlockSpec((G, GDN_HEADS, GDN_DK, GDN_DV), lambda i, c: (i, 0, 0, 0)),
                   pl.BlockSpec((G, GDN_CONV - 1, GDN_CONV_DIM), lambda i, c: (i, 0, 0))],
        out_shape=[jax.ShapeDtypeStruct((b, t, GDN_VAL), F32),
                   jax.ShapeDtypeStruct((b, GDN_HEADS, GDN_DK, GDN_DV), F32),
                   jax.ShapeDtypeStruct((b, GDN_CONV - 1, GDN_CONV_DIM), F32)],
        scratch_shapes=[pltpu.VMEM((G, C + SUBLANES, GDN_CONV_DIM), F32),
                        pltpu.VMEM((G, GDN_HEADS, GDN_DK, GDN_DV), F32)],
        compiler_params=_cparams(("parallel", "arbitrary")),
        name="gdn",
    )(p3, p3, p3, p3, p3, conv_w, par, nw, conv_state, state)


def _ssd_kernel(x_ref, b_ref, c_ref, z_ref, sm_ref, cw_ref, cb_ref, par_ref, dexp_ref, nw_ref, e_ref,
                cs_ref, h0_ref, o_ref, ho_ref, co_ref, ext_ref, h_ref, *, G, C, L, NC):
    ci = pl.program_id(1)
    hist = SSM_CONV - 1
    nbc = SSM_GROUPS * SSM_DSTATE

    @pl.when(ci == 0)
    def _():
        h_ref[...] = h0_ref[...].reshape(h_ref.shape)
        ext_ref[...] = jnp.zeros(ext_ref.shape, F32)
        for g in range(G):
            ext_ref[g, SUBLANES - hist:SUBLANES, :] = cs_ref[g]

    causal, _ = _tri_masks(C)
    tril = jnp.where(causal, 1.0, 0.0).astype(BF16)
    triu = jnp.where(lax.broadcasted_iota(jnp.int32, (C, C), 0)
                     <= lax.broadcasted_iota(jnp.int32, (C, C), 1), 1.0, 0.0).astype(BF16)
    rows = lax.broadcasted_iota(jnp.int32, (C, 1), 0)
    valid = rows < L
    lane = lax.broadcasted_iota(jnp.int32, (1, LANES), 1)
    dt_lanes = (lane >= SM_DT) & (lane < SM_DT + SSM_HEADS)
    half = [jnp.where(lane < SSM_HEADDIM, 1.0, 0.0).astype(F32), jnp.where(lane >= SSM_HEADDIM, 1.0, 0.0).astype(F32)]
    dt_bias = par_ref[0:1, :]
    neg_a = -jnp.exp(par_ref[1:2, :])
    cw = cw_ref[...]
    cb = cb_ref[...]
    expand = e_ref[...]
    gw = SSM_INNER // SSM_GROUPS

    pre = []
    for g in range(G):
        ext_ref[g, SUBLANES:SUBLANES + C, 0:SSM_INNER] = x_ref[g]
        ext_ref[g, SUBLANES:SUBLANES + C, SSM_INNER:SSM_INNER + nbc] = b_ref[g]
        ext_ref[g, SUBLANES:SUBLANES + C, SSM_INNER + nbc:] = c_ref[g]
        xbc = _silu(_conv_from_ext(ext_ref, g, cw, C, SSM_CONV) + cb)
        new_hist = ext_ref[g, SUBLANES + L - hist:SUBLANES + L, :]
        co_ref[g] = new_hist
        ext_ref[g, SUBLANES - hist:SUBLANES, :] = new_hist

        dt = jnp.where(valid & dt_lanes, _softplus(sm_ref[g] + dt_bias), 0.0)
        la_exp = _mask_dot_rhs(_split3(dt * neg_a), expand)
        dt_exp = _mask_dot_rhs(_split3(dt), expand)
        la3 = _split3(la_exp)
        acum = _mask_dot(tril, la3)
        acum_t = _mask_dot_tn(la3, triu)
        x = xbc[:, :SSM_INNER]
        xdt = x * dt_exp
        pre.append(dict(xbc=xbc, x=x, xdt=xdt, acum=acum, acum_t=acum_t, cin=jnp.exp(acum),
                        xdec=(xdt * jnp.exp(acum[C - 1:C, :] - acum)).astype(BF16)))

    cbs = []
    for g in range(G):
        xbc = pre[g]["xbc"]
        for gi in range(SSM_GROUPS):
            bg = xbc[:, SSM_INNER + gi * SSM_DSTATE:SSM_INNER + (gi + 1) * SSM_DSTATE].astype(BF16)
            cg = xbc[:, SSM_INNER + nbc + gi * SSM_DSTATE:SSM_INNER + nbc + (gi + 1) * SSM_DSTATE].astype(BF16)
            cbs.append((bg, cg, _dot_nt(cg, bg)))

    ys, h_new = [], []
    for g in range(G):
        d = pre[g]
        for p in range(SSM_HEADS // 2):
            bg, cg, cbm = cbs[g * SSM_GROUPS + p // 2]
            ps = slice(p * LANES, (p + 1) * LANES)
            xdt_p = d["xdt"][:, ps]
            hp = h_ref[g, p]
            y_p = _dot_nt(cg, hp.astype(BF16)) * d["cin"][:, ps]
            for e in range(2):
                head = 2 * p + e
                acol = d["acum"][:, head * SSM_HEADDIM:head * SSM_HEADDIM + 1]
                arow = d["acum_t"][head * SSM_HEADDIM:head * SSM_HEADDIM + 1, :]
                seg = jnp.where(causal, jnp.exp(jnp.minimum(acol - arow, 0.0)), 0.0)
                y_p = y_p + _dot((cbm * seg).astype(BF16), (xdt_p * half[e]).astype(BF16))
            ys.append(y_p)
            h_new.append(hp * jnp.exp(d["acum_t"][p * LANES:(p + 1) * LANES, C - 1:C]) + _dot_tn(d["xdec"][:, ps], bg))

    outs = []
    for g in range(G):
        d = pre[g]
        y = jnp.concatenate(ys[g * (SSM_HEADS // 2):(g + 1) * (SSM_HEADS // 2)], axis=1)
        y = (y + dexp_ref[...] * d["x"]) * _silu(z_ref[g])
        outs.append(jnp.concatenate([_rms(y[:, gi * gw:(gi + 1) * gw], nw_ref[:, gi * gw:(gi + 1) * gw])
                                     for gi in range(SSM_GROUPS)], axis=1))
    o_ref[...] = jnp.stack(outs).astype(o_ref.dtype)
    h_ref[...] = jnp.stack(h_new).reshape(h_ref.shape)

    @pl.when(ci == NC - 1)
    def _():
        ho_ref[...] = h_ref[...].reshape(ho_ref.shape)


def _ssd(p3, conv_w, conv_b, par, dexp, nw, expand, conv_state, state, layer, *, G, C, L):
    b, t, _ = p3.shape
    nc = t // C
    nbc = SSM_GROUPS * SSM_DSTATE
    blk = lambda col, w: pl.BlockSpec((G, C, w), lambda i, c: (i, c, col // w))
    fixed = lambda shape: pl.BlockSpec(shape, lambda i, c: (0,) * len(shape))
    kern = functools.partial(_ssd_kernel, G=G, C=C, L=L, NC=nc)
    return pl.pallas_call(
        kern,
        grid=(b // G, nc),
        in_specs=[blk(COL_SX, SSM_INNER), blk(COL_SB, nbc), blk(COL_SC, nbc), blk(COL_SZ, SSM_INNER),
                  blk(COL_SMALL, LANES),
                  fixed((SSM_CONV, SSM_CONV_DIM)), fixed((1, SSM_CONV_DIM)), fixed((SUBLANES, LANES)),
                  fixed((1, SSM_INNER)), fixed((1, SSM_INNER)), fixed((LANES, SSM_INNER)),
                  pl.BlockSpec((None, G, SSM_CONV - 1, SSM_CONV_DIM), lambda i, c: (layer, i, 0, 0)),
                  pl.BlockSpec((None, G, SSM_HEADS, SSM_HEADDIM, SSM_DSTATE), lambda i, c: (layer, i, 0, 0, 0))],
        out_specs=[pl.BlockSpec((G, C, SSM_INNER), lambda i, c: (i, c, 0)),
                   pl.BlockSpec((G, SSM_HEADS, SSM_HEADDIM, SSM_DSTATE), lambda i, c: (i, 0, 0, 0)),
                   pl.BlockSpec((G, SSM_CONV - 1, SSM_CONV_DIM), lambda i, c: (i, 0, 0))],
        out_shape=[jax.ShapeDtypeStruct((b, t, SSM_INNER), F32),
                   jax.ShapeDtypeStruct((b, SSM_HEADS, SSM_HEADDIM, SSM_DSTATE), F32),
                   jax.ShapeDtypeStruct((b, SSM_CONV - 1, SSM_CONV_DIM), F32)],
        scratch_shapes=[pltpu.VMEM((G, C + SUBLANES, SSM_CONV_DIM), F32),
                        pltpu.VMEM((G, SSM_HEADS // 2, LANES, SSM_DSTATE), F32)],
        compiler_params=_cparams(("parallel", "arbitrary")),
        name="ssd",
    )(p3, p3, p3, p3, p3, conv_w, conv_b, par, dexp, nw, expand, conv_state, state)


def _split_dot(a, b_bf16):
    hi = a.astype(BF16)
    lo = (a - hi.astype(F32)).astype(BF16)
    return _dot(hi, b_bf16) + _dot(lo, b_bf16)


def _qknorm_kernel(q_ref, k_ref, v_ref, qw_ref, kw_ref, ones_ref, oq_ref, ok_ref, km_ref, qt_ref, vt_ref):
    ones = ones_ref[...]
    q = q_ref[...]
    k = k_ref[...]
    inv = 1.0 / ATTN_HD
    qn = q * lax.rsqrt(_split_dot(q * q, ones) * inv + EPS) * qw_ref[...]
    oq_ref[...] = qn
    kn = k * lax.rsqrt(_split_dot(k * k, ones) * inv + EPS) * kw_ref[...]
    ok_ref[...] = kn
    km_ref[0] = jnp.mean(kn, axis=0, keepdims=True)
    qt_ref[...] = (qn * (ATTN_HD ** -0.5)).T
    vt_ref[...] = v_ref[...].T.astype(BF16)


def _qknorm(p2, qw, kw, ones, tm):
    m = p2.shape[0]
    nt = m // tm
    col = lambda c: pl.BlockSpec((tm, ATTN_WIDTH), lambda i: (i, c // ATTN_WIDTH))
    return pl.pallas_call(
        _qknorm_kernel,
        grid=(nt,),
        in_specs=[col(COL_AQ), col(COL_AK), col(COL_AV),
                  pl.BlockSpec((1, ATTN_WIDTH), lambda i: (0, 0)),
                  pl.BlockSpec((1, ATTN_WIDTH), lambda i: (0, 0)),
                  pl.BlockSpec((ATTN_WIDTH, ATTN_WIDTH), lambda i: (0, 0))],
        out_specs=[pl.BlockSpec((tm, ATTN_WIDTH), lambda i: (i, 0)),
                   pl.BlockSpec((tm, ATTN_WIDTH), lambda i: (i, 0)),
                   pl.BlockSpec((1, 1, ATTN_WIDTH), lambda i: (i, 0, 0)),
                   pl.BlockSpec((ATTN_WIDTH, tm), lambda i: (0, i)),
                   pl.BlockSpec((ATTN_WIDTH, tm), lambda i: (0, i))],
        out_shape=[jax.ShapeDtypeStruct((m, ATTN_WIDTH), F32),
                   jax.ShapeDtypeStruct((m, ATTN_WIDTH), F32),
                   jax.ShapeDtypeStruct((nt, 1, ATTN_WIDTH), F32),
                   jax.ShapeDtypeStruct((ATTN_WIDTH, m), F32),
                   jax.ShapeDtypeStruct((ATTN_WIDTH, m), BF16)],
        compiler_params=_cparams(("parallel",)),
        name="qknorm",
    )(p2, p2, p2, qw, kw, ones)


def _top_blocks_neg(sc, n_valid):
    blk = lax.broadcasted_iota(jnp.int32, (sc.shape[0], 1), 0)
    sc = jnp.where(blk < n_valid, sc, NEG)
    sel = jnp.zeros(sc.shape, F32)
    for _ in range(MOBA_TOPK):
        mx = jnp.max(sc, axis=0, keepdims=True)
        idx = jnp.min(jnp.where(sc == mx, blk, sc.shape[0]), axis=0, keepdims=True)
        pick = blk == idx
        sel = jnp.where(pick, 1.0, sel)
        sc = jnp.where(pick, LOWEST, sc)
    return jnp.where((blk < n_valid) & (sel > 0.0), 0.0, NEG)


def _moba_kernel(jq_tab, n_tab, qt_ref, k_ref, vt_ref, km_ref, bias_ref, o_ref,
                 qm_ref, sel_ref, m_ref, l_ref, acc_ref):
    s = pl.program_id(1)
    jq = jq_tab[s]
    n = n_tab[s]
    row = lax.broadcasted_iota(jnp.int32, (LANES, 1), 0)
    half = [row < ATTN_HD, row >= ATTN_HD]

    @pl.when(n == 0)
    def _():
        qt = qt_ref[...]
        km = km_ref[0]
        for h in range(ATTN_HEADS):
            ps = slice((h // 2) * LANES, (h // 2 + 1) * LANES)
            qm = jnp.where(half[h % 2], qt[ps, :], 0.0)
            qm_ref[h] = qm.astype(BF16)
            sel_ref[h] = _top_blocks_neg(_dot3(_split2(km[:, ps]), _split2(qm)), jq)
        m_ref[...] = jnp.full(m_ref.shape, NEG, F32)
        l_ref[...] = jnp.zeros(l_ref.shape, F32)
        acc_ref[...] = jnp.zeros(acc_ref.shape, F32)

    d = jq - n
    dd = jnp.minimum(d, BIAS_ND - 1)
    kb = k_ref[0].astype(BF16)
    vt = vt_ref[...]
    ones = jnp.ones((2 * SUBLANES, MOBA_BLOCK), BF16)
    m_old = m_ref[...]
    l_old = l_ref[...]
    m_new, l_new, acc_new = [], [], []
    for h in range(ATTN_HEADS):
        ps = slice((h // 2) * LANES, (h // 2 + 1) * LANES)
        rowneg = jnp.where(d == 0, 0.0, sel_ref[h, pl.ds(n, 1), :])
        lg = _dot(kb[:, ps], qm_ref[h]) + bias_ref[h, dd] + rowneg
        mo = m_old[h:h + 1, :]
        mn = jnp.maximum(mo, jnp.max(lg, axis=0, keepdims=True))
        alpha = jnp.exp(mo - mn)
        pt = jnp.exp(lg - mn).astype(BF16)
        lhs = jnp.concatenate([vt[h * ATTN_HD:(h + 1) * ATTN_HD, :], ones], axis=0)
        r = _dot(lhs, pt)
        m_new.append(mn)
        l_new.append(l_old[h:h + 1, :] * alpha + r[ATTN_HD:ATTN_HD + 1, :])
        acc_new.append(acc_ref[h] * alpha + r[0:ATTN_HD, :])
    m_ref[...] = jnp.concatenate(m_new, axis=0)
    l_ref[...] = jnp.concatenate(l_new, axis=0)
    acc_ref[...] = jnp.stack(acc_new)

    @pl.when(d == 0)
    def _():
        l = l_ref[...]
        ot = jnp.concatenate([acc_ref[h] / l[h:h + 1, :] for h in range(ATTN_HEADS)], axis=0)
        o_ref[0] = ot.T.astype(o_ref.dtype)


def _moba_prompt(qt, ak, vt, kmean, bias_tiles, b):
    _, t, _ = ak.shape
    nb = t // MOBA_BLOCK
    jq_tab = np.concatenate([np.full(j + 1, j) for j in range(nb)]).astype(np.int32)
    n_tab = np.concatenate([np.arange(j + 1) for j in range(nb)]).astype(np.int32)
    nsteps = int(jq_tab.shape[0])
    gs = pltpu.PrefetchScalarGridSpec(
        num_scalar_prefetch=2,
        grid=(b, nsteps),
        in_specs=[pl.BlockSpec((ATTN_WIDTH, MOBA_BLOCK), lambda i, s, jq, nn: (0, i * nb + jq[s])),
                  pl.BlockSpec((1, MOBA_BLOCK, ATTN_WIDTH), lambda i, s, jq, nn: (i, nn[s], 0)),
                  pl.BlockSpec((ATTN_WIDTH, MOBA_BLOCK), lambda i, s, jq, nn: (0, i * nb + nn[s])),
                  pl.BlockSpec((1, LANES, ATTN_WIDTH), lambda i, s, jq, nn: (i, 0, 0)),
                  pl.BlockSpec((ATTN_HEADS, BIAS_ND, MOBA_BLOCK, MOBA_BLOCK), lambda i, s, jq, nn: (0, 0, 0, 0))],
        out_specs=pl.BlockSpec((1, MOBA_BLOCK, ATTN_WIDTH), lambda i, s, jq, nn: (i, jq[s], 0)),
        scratch_shapes=[pltpu.VMEM((ATTN_HEADS, LANES, MOBA_BLOCK), BF16),
                        pltpu.VMEM((ATTN_HEADS, LANES, MOBA_BLOCK), F32),
                        pltpu.VMEM((ATTN_HEADS, MOBA_BLOCK), F32),
                        pltpu.VMEM((ATTN_HEADS, MOBA_BLOCK), F32),
                        pltpu.VMEM((ATTN_HEADS, ATTN_HD, MOBA_BLOCK), F32)])
    return pl.pallas_call(
        _moba_kernel,
        grid_spec=gs,
        out_shape=jax.ShapeDtypeStruct((b, t, ATTN_WIDTH), F32),
        compiler_params=_cparams(("parallel", "arbitrary")),
        name="moba_prompt",
    )(jnp.asarray(jq_tab), jnp.asarray(n_tab), qt, ak, vt, kmean, bias_tiles)


def _moba_sample_kernel(pt_ref, q_ref, kn_ref, vn_ref, bias_ref, bown_ref, hm_ref, pick_ref, ck_hbm, cv_hbm,
                        o_ref, kbuf, vbuf, sem, *, layer, npages, page, nseq, nq):
    b = pl.program_id(0)
    slot = lax.rem(b, 2)

    def page_copies(seq, sl):
        out = []
        for p in range(npages):
            pg = pt_ref[seq, p]
            out.append(pltpu.make_async_copy(ck_hbm.at[layer, pg], kbuf.at[sl, pl.ds(p * page, page)], sem.at[0, sl]))
            out.append(pltpu.make_async_copy(cv_hbm.at[layer, pg], vbuf.at[sl, pl.ds(p * page, page)], sem.at[1, sl]))
        return out

    @pl.when(b == 0)
    def _():
        for cp in page_copies(0, 0):
            cp.start()

    @pl.when(b + 1 < nseq)
    def _():
        for cp in page_copies(b + 1, 1 - slot):
            cp.start()

    for cp in page_copies(b, slot):
        cp.wait()

    past = npages * page
    nblk = past // MOBA_BLOCK
    nrep = ATTN_HEADS * nq
    hm = hm_ref[...]
    q = q_ref[0] * (ATTN_HD ** -0.5)
    qrep = _dot_tn(pick_ref[...], q, HI) * hm
    kp = kbuf[slot]
    vp = vbuf[slot]
    kmean = jnp.mean(kp.reshape(nblk, MOBA_BLOCK, ATTN_WIDTH), axis=1)
    sc = _dot_nt(kmean, qrep, HI)
    blk = lax.broadcasted_iota(jnp.int32, (nblk, 1), 0)
    sel = jnp.zeros(sc.shape, F32)
    for _ in range(min(MOBA_TOPK, nblk)):
        mx = jnp.max(sc, axis=0, keepdims=True)
        idx = jnp.min(jnp.where(sc == mx, blk, nblk), axis=0, keepdims=True)
        pick = blk == idx
        sel = jnp.where(pick, 1.0, sel)
        sc = jnp.where(pick, LOWEST, sc)
    selk = jnp.broadcast_to(sel[:, None, :], (nblk, MOBA_BLOCK, nrep)).reshape(past, nrep) > 0.0

    qb = qrep.astype(BF16)
    lg = jnp.where(selk, _dot_nt(kp.astype(BF16), qb) + bias_ref[...], NEG)
    lo = _dot_nt(kn_ref[0].astype(BF16), qb) + bown_ref[...]
    m = jnp.maximum(jnp.max(lg, axis=0, keepdims=True), jnp.max(lo, axis=0, keepdims=True))
    pp = jnp.where(selk, jnp.exp(lg - m), 0.0)
    po = jnp.exp(lo - m)
    ones = jnp.ones((past, LANES), F32)
    den = _dot_tn(pp, ones, HI) + _dot_tn(po, ones[0:SUBLANES], HI)
    r = _dot_tn(pp.astype(BF16), vp.astype(BF16)) + _dot_tn(po.astype(BF16), vn_ref[0].astype(BF16))
    r = r / den[:, 0:1] * hm
    o_ref[0] = _dot(pick_ref[...], r, HI).astype(o_ref.dtype)


def _moba_sample(page_table, aq, ak, av_p3, bias_t, bias_own, head_mask, pick, cache_k, cache_v, layer, nq):
    nseq = aq.shape[0]
    npages = page_table.shape[1]
    page = cache_k.shape[2]
    past = npages * page
    nrep = ATTN_HEADS * nq
    kern = functools.partial(_moba_sample_kernel, layer=layer, npages=npages, page=page, nseq=nseq, nq=nq)
    gs = pltpu.PrefetchScalarGridSpec(
        num_scalar_prefetch=1,
        grid=(nseq,),
        in_specs=[pl.BlockSpec((1, SUBLANES, ATTN_WIDTH), lambda i, pt: (i, 0, 0)),
                  pl.BlockSpec((1, SUBLANES, ATTN_WIDTH), lambda i, pt: (i, 0, 0)),
                  pl.BlockSpec((1, SUBLANES, ATTN_WIDTH), lambda i, pt: (i, 0, COL_AV // ATTN_WIDTH)),
                  pl.BlockSpec((past, nrep), lambda i, pt: (0, 0)),
                  pl.BlockSpec((SUBLANES, nrep), lambda i, pt: (0, 0)),
                  pl.BlockSpec((nrep, ATTN_WIDTH), lambda i, pt: (0, 0)),
                  pl.BlockSpec((SUBLANES, nrep), lambda i, pt: (0, 0)),
                  pl.BlockSpec(memory_space=pl.ANY),
                  pl.BlockSpec(memory_space=pl.ANY)],
        out_specs=pl.BlockSpec((1, SUBLANES, ATTN_WIDTH), lambda i, pt: (i, 0, 0)),
        scratch_shapes=[pltpu.VMEM((2, past, ATTN_WIDTH), F32),
                        pltpu.VMEM((2, past, ATTN_WIDTH), F32),
                        pltpu.SemaphoreType.DMA((2, 2))])
    return pl.pallas_call(
        kern,
        grid_spec=gs,
        out_shape=jax.ShapeDtypeStruct((nseq, SUBLANES, ATTN_WIDTH), F32),
        compiler_params=_cparams(("arbitrary",)),
        name="moba_sample",
    )(page_table, aq, ak, av_p3, bias_t, bias_own, head_mask, pick, cache_k, cache_v)


def _merge_kernel(x_ref, og_ref, oa_ref, os_ref, g0_ref, g1_ref, g2_ref, wg_ref, wa_ref, ws_ref, wo_ref, o_ref):
    merged = (_sigmoid(g0_ref[...]) * _dot(og_ref[...].astype(BF16), wg_ref[...])
              + _sigmoid(g1_ref[...]) * _dot(oa_ref[...].astype(BF16), wa_ref[...])
              + _sigmoid(g2_ref[...]) * _dot(os_ref[...].astype(BF16), ws_ref[...]))
    o_ref[...] = x_ref[...] + _dot(merged.astype(BF16), wo_ref[...])


def _merge(x, o_gdn, o_att, o_ssm, p2, wg, wa, ws, wo, tm):
    m = x.shape[0]
    row = lambda w, cb=0: pl.BlockSpec((tm, w), lambda i: (i, cb))
    fixed = lambda a, bb: pl.BlockSpec((a, bb), lambda i: (0, 0))
    return pl.pallas_call(
        _merge_kernel,
        grid=(m // tm,),
        in_specs=[row(D_MODEL), row(GDN_VAL), row(ATTN_WIDTH), row(SSM_INNER),
                  row(D_MODEL, COL_G0 // D_MODEL), row(D_MODEL, COL_G1 // D_MODEL), row(D_MODEL, COL_G2 // D_MODEL),
                  fixed(GDN_VAL, D_MODEL), fixed(ATTN_WIDTH, D_MODEL), fixed(SSM_INNER, D_MODEL),
                  fixed(D_MODEL, D_MODEL)],
        out_specs=row(D_MODEL),
        out_shape=jax.ShapeDtypeStruct((m, D_MODEL), F32),
        compiler_params=_cparams(("parallel",)),
        name="merge",
    )(x, o_gdn, o_att, o_ssm, p2, p2, p2, wg, wa, ws, wo)


FFN_HALO = 16


def _ffn_prompt_kernel(x_ref, xh_ref, nw_ref, upg_ref, upv_ref, cwg_ref, cwv_ref, cbg_ref, cbv_ref, dn_ref,
                       o_ref, tg_ref, tv_ref, h_ref, ug_ref, uv_ref, *, tm, seq_tiles):
    i = pl.program_id(0)
    j = pl.program_id(1)

    @pl.when(j == 0)
    def _():
        x = x_ref[...]
        h_ref[FFN_HALO:, :] = _rms(x, nw_ref[...]).astype(BF16)
        h_ref[0:FFN_HALO, :] = _rms(xh_ref[...], nw_ref[...]).astype(BF16)
        o_ref[...] = x

    first = lax.rem(i, seq_tiles) == 0
    rows = lax.broadcasted_iota(jnp.int32, (tm + FFN_HALO, 1), 0)
    keep = jnp.where(first & (rows < FFN_HALO), 0.0, 1.0)
    h = h_ref[...]
    ug_ref[...] = _dot(h, upg_ref[...]) * keep
    uv_ref[...] = _dot(h, upv_ref[...]) * keep

    def conv(u_ref, cw_ref, cb_ref):
        acc = cb_ref[...]
        for k in range(FFN_CONV):
            off = FFN_HALO - (FFN_CONV - 1) + k
            acc = acc + cw_ref[k:k + 1, :] * u_ref[off:off + tm, :]
        return acc

    act = _silu(conv(ug_ref, cwg_ref, cbg_ref)) * conv(uv_ref, cwv_ref, cbv_ref)
    o_ref[...] += _dot(act.astype(BF16), dn_ref[...])
    tg_ref[0] = ug_ref[tm + FFN_HALO - SUBLANES:, :]
    tv_ref[0] = uv_ref[tm + FFN_HALO - SUBLANES:, :]


def _ffn_prompt(x, nw, up, cw, cb, dn, tm, seq_len):
    m = x.shape[0]
    nt = m // tm
    nj = D_FF // FFN_TN
    hb = tm // FFN_HALO
    kern = functools.partial(_ffn_prompt_kernel, tm=tm, seq_tiles=seq_len // tm)
    vec = lambda r, off: pl.BlockSpec((r, FFN_TN), lambda i, j: (0, j + off))
    return pl.pallas_call(
        kern,
        grid=(nt, nj),
        in_specs=[pl.BlockSpec((tm, D_MODEL), lambda i, j: (i, 0)),
                  pl.BlockSpec((FFN_HALO, D_MODEL), lambda i, j: (jnp.maximum(i * hb - 1, 0), 0)),
                  pl.BlockSpec((1, D_MODEL), lambda i, j: (0, 0)),
                  pl.BlockSpec((D_MODEL, FFN_TN), lambda i, j: (0, j)),
                  pl.BlockSpec((D_MODEL, FFN_TN), lambda i, j: (0, j + nj)),
                  vec(FFN_CONV, 0), vec(FFN_CONV, nj), vec(1, 0), vec(1, nj),
                  pl.BlockSpec((FFN_TN, D_MODEL), lambda i, j: (j, 0))],
        out_specs=[pl.BlockSpec((tm, D_MODEL), lambda i, j: (i, 0)),
                   pl.BlockSpec((1, SUBLANES, FFN_TN), lambda i, j: (i, 0, j)),
                   pl.BlockSpec((1, SUBLANES, FFN_TN), lambda i, j: (i, 0, j))],
        out_shape=[jax.ShapeDtypeStruct((m, D_MODEL), F32),
                   jax.ShapeDtypeStruct((nt, SUBLANES, D_FF), F32),
                   jax.ShapeDtypeStruct((nt, SUBLANES, D_FF), F32)],
        scratch_shapes=[pltpu.VMEM((tm + FFN_HALO, D_MODEL), BF16),
                        pltpu.VMEM((tm + FFN_HALO, FFN_TN), F32),
                        pltpu.VMEM((tm + FFN_HALO, FFN_TN), F32)],
        compiler_params=_cparams(("parallel", "arbitrary")),
        name="ffn_prompt",
    )(x, x, nw, up, up, cw, cw, cb, cb, dn)


def _ffn_sample_kernel(x_ref, nw_ref, upg_ref, upv_ref, cwg_ref, cwv_ref, cbg_ref, cbv_ref, dn_ref, sg_ref, sv_ref,
                       o_ref, ng_ref, nv_ref, h_ref, *, nb):
    j = pl.program_id(0)

    @pl.when(j == 0)
    def _():
        x = x_ref[...]
        h_ref[...] = _rms(x, nw_ref[...]).astype(BF16)
        o_ref[...] = x

    h = h_ref[...]
    hist = FFN_CONV - 1

    def conv(u, s_ref, cw_ref, cb_ref, n_ref):
        full = jnp.concatenate([s_ref[...], u], axis=0)
        rows = u.shape[0]
        acc = cb_ref[...]
        for k in range(FFN_CONV):
            acc = acc + cw_ref[k:k + 1, :] * full[k * nb:k * nb + rows, :]
        n_ref[...] = full[rows:, :]
        return acc

    cg = conv(_dot(h, upg_ref[...]), sg_ref, cwg_ref, cbg_ref, ng_ref)
    cv = conv(_dot(h, upv_ref[...]), sv_ref, cwv_ref, cbv_ref, nv_ref)
    o_ref[...] += _dot((_silu(cg) * cv).astype(BF16), dn_ref[...])


def _ffn_sample(x, nw, up, cw, cb, dn, state_g, state_v, nb):
    m = x.shape[0]
    nj = D_FF // FFN_TN
    srows = (FFN_CONV - 1) * nb
    kern = functools.partial(_ffn_sample_kernel, nb=nb)
    vec = lambda r, off: pl.BlockSpec((r, FFN_TN), lambda j: (0, j + off))
    return pl.pallas_call(
        kern,
        grid=(nj,),
        in_specs=[pl.BlockSpec((m, D_MODEL), lambda j: (0, 0)),
                  pl.BlockSpec((1, D_MODEL), lambda j: (0, 0)),
                  pl.BlockSpec((D_MODEL, FFN_TN), lambda j: (0, j)),
                  pl.BlockSpec((D_MODEL, FFN_TN), lambda j: (0, j + nj)),
                  vec(FFN_CONV, 0), vec(FFN_CONV, nj), vec(1, 0), vec(1, nj),
                  pl.BlockSpec((FFN_TN, D_MODEL), lambda j: (j, 0)),
                  pl.BlockSpec((srows, FFN_TN), lambda j: (0, j)),
                  pl.BlockSpec((srows, FFN_TN), lambda j: (0, j))],
        out_specs=[pl.BlockSpec((m, D_MODEL), lambda j: (0, 0)),
                   pl.BlockSpec((srows, FFN_TN), lambda j: (0, j)),
                   pl.BlockSpec((srows, FFN_TN), lambda j: (0, j))],
        out_shape=[jax.ShapeDtypeStruct((m, D_MODEL), F32),
                   jax.ShapeDtypeStruct((srows, D_FF), F32),
                   jax.ShapeDtypeStruct((srows, D_FF), F32)],
        scratch_shapes=[pltpu.VMEM((m, D_MODEL), BF16)],
        compiler_params=_cparams(("arbitrary",)),
        name="ffn_sample",
    )(x, nw, up, up, cw, cw, cb, cb, dn, state_g, state_v)


def _t5_bucket(rel):
    n = jnp.maximum(rel, 0)
    exact = NUM_BUCKETS // 2
    nf = jnp.maximum(n, 1).astype(F32)
    large = exact + (jnp.log(nf / exact) / math.log(MAX_DISTANCE / exact) * (NUM_BUCKETS - exact)).astype(jnp.int32)
    return jnp.where(n < exact, n, jnp.minimum(large, NUM_BUCKETS - 1))


def _bias_by_distance(rel_bias, max_rel):
    return rel_bias.astype(F32)[_t5_bucket(jnp.arange(max_rel)), :].T


def _prompt_bias_tiles(rel_bias):
    blk = MOBA_BLOCK
    tab = _bias_by_distance(rel_bias, (BIAS_ND + 1) * blk)
    tab = jnp.concatenate([jnp.zeros((ATTN_HEADS, blk), F32), tab], axis=1)
    causal = np.arange(blk)[:, None] <= np.arange(blk)[None, :]
    tiles = []
    for d in range(BIAS_ND):
        w = tab[:, d * blk + 1:(d + 2) * blk]
        w = jnp.concatenate([w, jnp.zeros((ATTN_HEADS, 1), F32)], axis=1)
        rows = jnp.tile(w, (1, blk))[:, :blk * (2 * blk - 1)].reshape(ATTN_HEADS, blk, 2 * blk - 1)
        tile = rows[:, :, blk - 1:]
        tiles.append(jnp.where(causal, tile, NEG) if d == 0 else tile)
    return jnp.stack(tiles, axis=1)


def _sample_bias_tables(rel_bias, past_len, nq):
    tab = _bias_by_distance(rel_bias, past_len + nq)
    kpos = np.arange(past_len)[:, None]
    rel = past_len + np.arange(nq)[None, :] - kpos
    past_t = jnp.transpose(tab[:, rel], (1, 0, 2)).reshape(past_len, ATTN_HEADS * nq)
    ko = np.arange(SUBLANES)[:, None]
    relo = np.arange(nq)[None, :] - ko
    own = jnp.transpose(tab[:, np.maximum(relo, 0)], (1, 0, 2))
    ok = jnp.asarray(((relo >= 0) & (ko < nq))[:, None, :])
    own_t = jnp.where(ok, own, NEG).reshape(SUBLANES, ATTN_HEADS * nq)
    return past_t, own_t


def _permute_w_in(w):
    cuts = np.cumsum((0,) + IN_SIZES)
    sec = lambda k, lo=0, hi=None: w[:, cuts[k] + lo: (cuts[k + 1] if hi is None else cuts[k] + hi)]
    nbc = SSM_GROUPS * SSM_DSTATE
    small = jnp.concatenate([sec(2), sec(3), sec(9),
                             jnp.zeros((w.shape[0], LANES - 2 * GDN_HEADS - SSM_HEADS), w.dtype)], axis=1)
    parts = [sec(7), sec(8, 0, SSM_INNER),
             sec(10, 0, D_MODEL), sec(10, D_MODEL, 2 * D_MODEL), sec(10, 2 * D_MODEL, 3 * D_MODEL),
             sec(0, 0, GDN_KEY), sec(0, GDN_KEY, 2 * GDN_KEY), sec(0, 2 * GDN_KEY, GDN_CONV_DIM), sec(1),
             sec(4), sec(5), sec(6),
             sec(8, SSM_INNER, SSM_INNER + nbc), sec(8, SSM_INNER + nbc, SSM_CONV_DIM), small]
    return jnp.concatenate(parts, axis=1).astype(BF16)


def _lane_row(vals, offset):
    return jnp.zeros((LANES,), F32).at[offset:offset + vals.shape[0]].set(vals.astype(F32))


def _pick_tm(m, cap):
    tm = cap
    while m % tm:
        tm //= 2
    return tm


def kernel(x_prompt, x_sample, cache_k, cache_v, page_table, state_gdn, state_gdn_conv, state_ssm, state_ssm_conv, state_ffn_conv, rel_bias, norm_mix, w_in, gdn_conv_w, gdn_A_log, gdn_dt_bias, gdn_norm, q_norm, k_norm, ssm_conv_w, ssm_conv_b, ssm_A_log, ssm_dt_bias, ssm_D, ssm_norm, w_br_gdn, w_br_attn, w_br_ssm, w_out, norm_ffn, ffn_up, ffn_conv_w, ffn_conv_b, ffn_down):
    depth = w_in.shape[0]
    bp, seq, _ = x_prompt.shape
    bd, nq, _ = x_sample.shape
    npool, page = cache_k.shape[1], cache_k.shape[2]
    npages = page_table.shape[1]
    past_len = npages * page
    nblk = seq // MOBA_BLOCK
    assert seq % MOBA_BLOCK == 0 and nblk <= LANES and past_len % MOBA_BLOCK == 0
    assert nq <= SUBLANES and bd % SUBLANES == 0

    ck = cache_k.reshape(depth, npool, page, ATTN_WIDTH)
    cv = cache_v.reshape(depth, npool, page, ATTN_WIDTH)
    bias_tiles = _prompt_bias_tiles(rel_bias)
    bias_past, bias_own = _sample_bias_tables(rel_bias, past_len, nq)
    nrep = ATTN_HEADS * nq
    col_head = np.arange(nrep) // nq
    head_mask = jnp.asarray((np.arange(ATTN_WIDTH)[None, :] // ATTN_HD == col_head[:, None]).astype(np.float32))
    pick = jnp.asarray((np.arange(SUBLANES)[:, None] == (np.arange(nrep) % nq)[None, :]).astype(np.float32))
    ones_bd = jnp.asarray((np.arange(ATTN_WIDTH)[:, None] // ATTN_HD
                           == np.arange(ATTN_WIDTH)[None, :] // ATTN_HD).astype(np.float32)).astype(BF16)
    expand = jnp.asarray((np.arange(LANES)[:, None] - SM_DT
                          == np.arange(SSM_INNER)[None, :] // SSM_HEADDIM).astype(np.float32)).astype(BF16)
    zeros_gdn_state = jnp.zeros((1, bp, GDN_HEADS, GDN_DK, GDN_DV), F32)
    zeros_gdn_conv = jnp.zeros((1, bp, GDN_CONV - 1, GDN_CONV_DIM), F32)
    zeros_ssm_state = jnp.zeros((1, bp, SSM_HEADS, SSM_HEADDIM, SSM_DSTATE), F32)
    zeros_ssm_conv = jnp.zeros((1, bp, SSM_CONV - 1, SSM_CONV_DIM), F32)

    mp = bp * seq
    tm_p = _pick_tm(mp, 1024)
    tm_ffn = _pick_tm(seq, 512)
    chunk = math.gcd(seq, 64)
    ms = bd * SUBLANES
    yp = x_prompt.reshape(mp, D_MODEL)
    ys = x_sample
    out_p = [[] for _ in range(7)]
    out_s = [[] for _ in range(7)]

    for l in range(depth):
        w_perm = _permute_w_in(w_in[l])
        nmix = norm_mix[l].reshape(1, D_MODEL)
        gdn_par = jnp.zeros((SUBLANES, LANES), F32).at[0].set(_lane_row(gdn_dt_bias[l], SM_A)).at[1].set(
            _lane_row(gdn_A_log[l], SM_A))
        ssm_par = jnp.zeros((SUBLANES, LANES), F32).at[0].set(_lane_row(ssm_dt_bias[l], SM_DT)).at[1].set(
            _lane_row(ssm_A_log[l], SM_DT))
        gnw = gdn_norm[l].reshape(1, GDN_DV)
        qw = jnp.tile(q_norm[l], ATTN_HEADS).reshape(1, ATTN_WIDTH)
        kw = jnp.tile(k_norm[l], ATTN_HEADS).reshape(1, ATTN_WIDTH)
        scb = ssm_conv_b[l].reshape(1, SSM_CONV_DIM)
        dexp = jnp.repeat(ssm_D[l], SSM_HEADDIM).reshape(1, SSM_INNER)
        snw = ssm_norm[l].reshape(1, SSM_INNER)
        wg, wa, ws, wo = (w_br_gdn[l].astype(BF16), w_br_attn[l].astype(BF16), w_br_ssm[l].astype(BF16),
                          w_out[l].astype(BF16))
        nffn = norm_ffn[l].reshape(1, D_MODEL)
        up = ffn_up[l].astype(BF16)
        dn = ffn_down[l].astype(BF16)
        fcw = ffn_conv_w[l]
        fcb = ffn_conv_b[l].reshape(1, 2 * D_FF)

        p2 = _norm_proj(yp, nmix, w_perm, tm_p)
        p3 = p2.reshape(bp, seq, P_WIDTH)
        og, gs_new, gc_new = _gdn(p3, gdn_conv_w[l], gdn_par, gnw, zeros_gdn_conv, zeros_gdn_state, 0,
                                  G=bp, C=chunk, L=chunk)
        osm, hs_new, sc_new = _ssd(p3, ssm_conv_w[l], scb, ssm_par, dexp, snw, expand, zeros_ssm_conv,
                                   zeros_ssm_state, 0, G=bp, C=chunk, L=chunk)
        _, ak, km, qt, vt = _qknorm(p2, qw, kw, ones_bd, MOBA_BLOCK)
        kmean = jnp.pad(km.reshape(bp, nblk, ATTN_WIDTH), ((0, 0), (0, LANES - nblk), (0, 0)))
        oa = _moba_prompt(qt, ak.reshape(bp, seq, ATTN_WIDTH), vt, kmean, bias_tiles, bp)
        x1 = _merge(yp, og.reshape(mp, GDN_VAL), oa.reshape(mp, ATTN_WIDTH), osm.reshape(mp, SSM_INNER), p2,
                    wg, wa, ws, wo, _pick_tm(mp, 512))
        yp, tg, tv = _ffn_prompt(x1, nffn, up, fcw, fcb, dn, tm_ffn, seq)
        tpb = seq // tm_ffn
        last = np.arange(bp) * tpb + tpb - 1
        ffn_state = jnp.concatenate([tg[last], tv[last]], axis=-1)[:, SUBLANES - (FFN_CONV - 1):, :]
        for acc, tns in zip(out_p, (ak.reshape(bp, seq, ATTN_HEADS, ATTN_HD),
                                    p3[:, :, COL_AV:COL_AV + ATTN_WIDTH].reshape(bp, seq, ATTN_HEADS, ATTN_HD),
                                    gs_new, gc_new, hs_new, sc_new, ffn_state)):
            acc.append(tns)

        xs_pad = jnp.pad(ys, ((0, 0), (0, SUBLANES - nq), (0, 0))).reshape(ms, D_MODEL)
        q2 = _norm_proj(xs_pad, nmix, w_perm, _pick_tm(ms, 1024))
        q3 = q2.reshape(bd, SUBLANES, P_WIDTH)
        gsz = 2
        og_s, gs_s, gc_s = _gdn(q3, gdn_conv_w[l], gdn_par, gnw, state_gdn_conv, state_gdn, l,
                                G=gsz, C=SUBLANES, L=nq)
        os_s, hs_s, sc_s = _ssd(q3, ssm_conv_w[l], scb, ssm_par, dexp, snw, expand, state_ssm_conv, state_ssm, l,
                                G=gsz, C=SUBLANES, L=nq)
        aq_s, ak_s, _, _, _ = _qknorm(q2, qw, kw, ones_bd, _pick_tm(ms, MOBA_BLOCK))
        oa_s = _moba_sample(page_table, aq_s.reshape(bd, SUBLANES, ATTN_WIDTH), ak_s.reshape(bd, SUBLANES, ATTN_WIDTH),
                            q3, bias_past, bias_own, head_mask, pick, ck, cv, l, nq)
        x1_s = _merge(xs_pad, og_s.reshape(ms, GDN_VAL), oa_s.reshape(ms, ATTN_WIDTH), os_s.reshape(ms, SSM_INNER), q2,
                      wg, wa, ws, wo, _pick_tm(ms, 512))
        x1_t = jnp.transpose(x1_s.reshape(bd, SUBLANES, D_MODEL)[:, :nq], (1, 0, 2)).reshape(nq * bd, D_MODEL)
        st = jnp.transpose(state_ffn_conv[l], (1, 0, 2)).reshape((FFN_CONV - 1) * bd, 2 * D_FF)
        y_t, ng, nv = _ffn_sample(x1_t, nffn, up, fcw, fcb, dn, st[:, :D_FF], st[:, D_FF:], bd)
        ys = jnp.transpose(y_t.reshape(nq, bd, D_MODEL), (1, 0, 2))
        ffn_state_s = jnp.transpose(jnp.concatenate([ng, nv], axis=-1).reshape(FFN_CONV - 1, bd, 2 * D_FF), (1, 0, 2))
        for acc, tns in zip(out_s, (ak_s.reshape(bd, SUBLANES, ATTN_HEADS, ATTN_HD)[:, :nq],
                                    q3[:, :nq, COL_AV:COL_AV + ATTN_WIDTH].reshape(bd, nq, ATTN_HEADS, ATTN_HD),
                                    gs_s, gc_s, hs_s, sc_s, ffn_state_s)):
            acc.append(tns)

    k_p, v_p, gdn_p, gdn_conv_p, ssm_p, ssm_conv_p, ffn_conv_p = [jnp.stack(a) for a in out_p]
    k_s, v_s, gdn_s, gdn_conv_s, ssm_s, ssm_conv_s, ffn_conv_s = [jnp.stack(a) for a in out_s]
    return (yp.reshape(bp, seq, D_MODEL), ys, k_p, v_p, gdn_p, gdn_conv_p, ssm_p, ssm_conv_p, ffn_conv_p,
            k_s, v_s, gdn_s, gdn_conv_s, ssm_s, ssm_conv_s, ffn_conv_s)
```

```python
import functools
import math

import numpy as np
import jax
import jax.numpy as jnp
from jax import lax
from jax.experimental import pallas as pl
from jax.experimental.pallas import tpu as pltpu

F32 = jnp.float32
BF16 = jnp.bfloat16
HI = lax.Precision.HIGHEST

D_MODEL = 1024
GDN_HEADS, GDN_DK, GDN_DV, GDN_CONV = 4, 128, 128, 4
GDN_KEY = GDN_HEADS * GDN_DK
GDN_VAL = GDN_HEADS * GDN_DV
GDN_CONV_DIM = 2 * GDN_KEY + GDN_VAL
ATTN_HEADS, ATTN_HD = 8, 64
ATTN_WIDTH = ATTN_HEADS * ATTN_HD
MOBA_BLOCK, MOBA_TOPK = 256, 3
NUM_BUCKETS, MAX_DISTANCE = 32, 1024
SSM_INNER, SSM_HEADDIM, SSM_GROUPS, SSM_DSTATE, SSM_CONV = 1024, 64, 4, 128, 4
SSM_HEADS = SSM_INNER // SSM_HEADDIM
SSM_CONV_DIM = SSM_INNER + 2 * SSM_GROUPS * SSM_DSTATE
N_BRANCH = 3
D_FF = 11 * D_MODEL // 4
FFN_CONV = 3
IN_SIZES = (GDN_CONV_DIM, GDN_VAL, GDN_HEADS, GDN_HEADS, ATTN_WIDTH, ATTN_WIDTH, ATTN_WIDTH,
            SSM_INNER, SSM_CONV_DIM, SSM_HEADS, N_BRANCH * D_MODEL)
EPS = 1e-6
NEG = -1e30
LOWEST = -3e38

LANES = 128
SUBLANES = 8
VMEM_LIMIT = 56 * 1024 * 1024

COL_SZ, COL_SX, COL_G0, COL_G1, COL_G2 = 0, 1024, 2048, 3072, 4096
COL_GQ, COL_GK, COL_GV, COL_GZ = 5120, 5632, 6144, 6656
COL_AQ, COL_AK, COL_AV = 7168, 7680, 8192
COL_SB, COL_SC = 8704, 9216
COL_SMALL = 9728
P_WIDTH = COL_SMALL + LANES
SM_A, SM_B, SM_DT = 0, GDN_HEADS, 2 * GDN_HEADS
FFN_TN = 1408
BIAS_ND = 6

def _cparams(sem):
    return pltpu.CompilerParams(dimension_semantics=sem, vmem_limit_bytes=VMEM_LIMIT)


def _dot(a, b, prec=None):
    return lax.dot_general(a, b, (((1,), (0,)), ((), ())), precision=prec, preferred_element_type=F32)


def _dot_nt(a, b, prec=None):
    return lax.dot_general(a, b, (((1,), (1,)), ((), ())), precision=prec, preferred_element_type=F32)


def _dot_tn(a, b, prec=None):
    return lax.dot_general(a, b, (((0,), (0,)), ((), ())), precision=prec, preferred_element_type=F32)


def _sigmoid(x):
    return 1.0 / (1.0 + jnp.exp(-x))


def _silu(x):
    return x * _sigmoid(x)


def _softplus(x):
    return jnp.maximum(x, 0.0) + jnp.log1p(jnp.exp(-jnp.abs(x)))


def _rms(x, w):
    return x * lax.rsqrt(jnp.mean(x * x, axis=-1, keepdims=True) + EPS) * w


def _norm_proj_kernel(x_ref, nw_ref, w_ref, o_ref, h_ref):
    @pl.when(pl.program_id(1) == 0)
    def _():
        h_ref[...] = _rms(x_ref[...], nw_ref[...]).astype(BF16)

    o_ref[...] = _dot(h_ref[...], w_ref[...])


def _norm_proj(x, nw, w, tm):
    m = x.shape[0]
    tn = 1408
    return pl.pallas_call(
        _norm_proj_kernel,
        grid=(m // tm, P_WIDTH // tn),
        in_specs=[pl.BlockSpec((tm, D_MODEL), lambda i, j: (i, 0)),
                  pl.BlockSpec((1, D_MODEL), lambda i, j: (0, 0)),
                  pl.BlockSpec((D_MODEL, tn), lambda i, j: (0, j))],
        out_specs=pl.BlockSpec((tm, tn), lambda i, j: (i, j)),
        out_shape=jax.ShapeDtypeStruct((m, P_WIDTH), F32),
        scratch_shapes=[pltpu.VMEM((tm, D_MODEL), BF16)],
        compiler_params=_cparams(("parallel", "arbitrary")),
        name="norm_proj",
    )(x, nw, w)


def _conv_from_ext(ext_ref, g, cw, c, width):
    acc = None
    for k in range(width):
        off = SUBLANES - (width - 1) + k
        term = cw[k:k + 1, :] * ext_ref[g, off:off + c, :]
        acc = term if acc is None else acc + term
    return acc


def _tri_masks(c):
    r = lax.broadcasted_iota(jnp.int32, (c, c), 0)
    s = lax.broadcasted_iota(jnp.int32, (c, c), 1)
    return r >= s, r > s


def _split2(x):
    hi = x.astype(BF16)
    return hi, (x - hi.astype(F32)).astype(BF16)


def _split3(x):
    a = x.astype(BF16)
    r = x - a.astype(F32)
    b = r.astype(BF16)
    return a, b, (r - b.astype(F32)).astype(BF16)


def _dot3(a2, b2, dot=None):
    dot = _dot if dot is None else dot
    return dot(a2[0], b2[0]) + (dot(a2[0], b2[1]) + dot(a2[1], b2[0]))


def _mask_dot(mask, x3):
    return _dot(mask, x3[0]) + (_dot(mask, x3[1]) + _dot(mask, x3[2]))


def _mask_dot_rhs(x3, mask):
    return _dot(x3[0], mask) + (_dot(x3[1], mask) + _dot(x3[2], mask))


def _mask_dot_tn(x3, mask):
    return _dot_tn(x3[0], mask) + (_dot_tn(x3[1], mask) + _dot_tn(x3[2], mask))


def _unit_lower_inverses(n_list, c):
    r = lax.broadcasted_iota(jnp.int32, (c, c), 0)
    s = lax.broadcasted_iota(jnp.int32, (c, c), 1)
    eye = jnp.where(r == s, 1.0, 0.0).astype(F32)
    ts = [eye - n for n in n_list]
    p2 = [_split2(-n) for n in n_list]
    span = 2
    while span < c:
        p2 = [_split2(_dot3(q, q)) for q in p2]
        ts = [t + _dot3(_split2(t), q) for t, q in zip(ts, p2)]
        span *= 2
    return ts


def _gdn_kernel(q_ref, k_ref, v_ref, z_ref, sm_ref, cw_ref, par_ref, nw_ref, cs_ref, s0_ref,
                o_ref, so_ref, co_ref, ext_ref, s_ref, *, G, C, L, NC):
    ci = pl.program_id(1)
    hist = GDN_CONV - 1

    @pl.when(ci == 0)
    def _():
        s_ref[...] = s0_ref[...]
        ext_ref[...] = jnp.zeros(ext_ref.shape, F32)
        for g in range(G):
            ext_ref[g, SUBLANES - hist:SUBLANES, :] = cs_ref[g]

    causal, strict = _tri_masks(C)
    tril = jnp.where(causal, 1.0, 0.0).astype(BF16)
    triu = jnp.where(lax.broadcasted_iota(jnp.int32, (C, C), 0)
                     <= lax.broadcasted_iota(jnp.int32, (C, C), 1), 1.0, 0.0).astype(BF16)
    rows = lax.broadcasted_iota(jnp.int32, (C, 1), 0)
    valid = rows < L
    lane = lax.broadcasted_iota(jnp.int32, (1, LANES), 1)
    dt_bias = par_ref[0:1, :]
    neg_a = -jnp.exp(par_ref[1:2, :])
    cw = cw_ref[...]
    nw = nw_ref[...]

    hs = []
    for g in range(G):
        ext_ref[g, SUBLANES:SUBLANES + C, 0:GDN_KEY] = q_ref[g]
        ext_ref[g, SUBLANES:SUBLANES + C, GDN_KEY:2 * GDN_KEY] = k_ref[g]
        ext_ref[g, SUBLANES:SUBLANES + C, 2 * GDN_KEY:] = v_ref[g]
        xc = _silu(_conv_from_ext(ext_ref, g, cw, C, GDN_CONV))
        new_hist = ext_ref[g, SUBLANES + L - hist:SUBLANES + L, :]
        co_ref[g] = new_hist
        ext_ref[g, SUBLANES - hist:SUBLANES, :] = new_hist

        small = sm_ref[g]
        gmat = jnp.where(valid & (lane < SM_B), neg_a * _softplus(small + dt_bias), 0.0)
        bmat = jnp.where(valid, _sigmoid(small), 0.0)
        g3 = _split3(gmat)
        gc = _mask_dot(tril, g3)
        gct = _mask_dot_tn(g3, triu)
        z = z_ref[g]
        for h in range(GDN_HEADS):
            gcol = gc[:, SM_A + h:SM_A + h + 1]
            grow = gct[SM_A + h:SM_A + h + 1, :]
            glast = gc[C - 1:C, SM_A + h:SM_A + h + 1]
            bcol = bmat[:, SM_B + h:SM_B + h + 1]
            qh = xc[:, h * GDN_DK:(h + 1) * GDN_DK]
            kh = xc[:, GDN_KEY + h * GDN_DK:GDN_KEY + (h + 1) * GDN_DK]
            vh = xc[:, 2 * GDN_KEY + h * GDN_DV:2 * GDN_KEY + (h + 1) * GDN_DV]
            qh = qh * lax.rsqrt(jnp.sum(qh * qh, axis=-1, keepdims=True) + EPS) * (GDN_DK ** -0.5)
            kh = jnp.where(valid, kh * lax.rsqrt(jnp.sum(kh * kh, axis=-1, keepdims=True) + EPS), 0.0)
            vh = jnp.where(valid, vh, 0.0)
            ecol = jnp.exp(gcol)
            hs.append(dict(
                g=g, h=h, bcol=bcol, q=qh, k=kh, kb=kh.astype(BF16), glast=glast,
                gam=jnp.where(causal, jnp.exp(jnp.minimum(gcol - grow, 0.0)), 0.0),
                rhs=jnp.concatenate([vh * bcol, kh * (bcol * ecol)], axis=1),
                q_dec=(qh * ecol).astype(BF16), k_dec=(kh * jnp.exp(glast - gcol)).astype(BF16),
                z=z[:, h * GDN_DV:(h + 1) * GDN_DV], s=s_ref[g, h]))

    kks = [_dot_nt(d["kb"], d["kb"]) for d in hs]
    tinvs = _unit_lower_inverses([jnp.where(strict, d["bcol"] * kk * d["gam"], 0.0) for d, kk in zip(hs, kks)], C)
    us = [_dot3(_split2(t), _split2(d["rhs"])) for d, t in zip(hs, tinvs)]
    qks = [(_dot_nt(d["q"].astype(BF16), d["kb"]) * d["gam"]).astype(BF16) for d in hs]
    sbs = [d["s"].astype(BF16) for d in hs]
    ws = [u[:, :GDN_DV] - _dot(u[:, GDN_DV:].astype(BF16), sb) for u, sb in zip(us, sbs)]
    wbs = [w.astype(BF16) for w in ws]
    outs = [_dot(d["q_dec"], sb) + _dot(qk, wb) for d, sb, qk, wb in zip(hs, sbs, qks, wbs)]
    s_new = [d["s"] * jnp.exp(d["glast"]) + _dot_tn(d["k_dec"], wb) for d, wb in zip(hs, wbs)]
    outs = [_rms(o, nw) * _silu(d["z"]) for d, o in zip(hs, outs)]
    o_ref[...] = jnp.stack([jnp.concatenate(outs[g * GDN_HEADS:(g + 1) * GDN_HEADS], axis=1)
                            for g in range(G)]).astype(o_ref.dtype)
    s_ref[...] = jnp.stack(s_new).reshape(s_ref.shape)

    @pl.when(ci == NC - 1)
    def _():
        so_ref[...] = s_ref[...]


def _gdn(p3, conv_w, par, nw, conv_state, state, layer, *, G, C, L):
    b, t, _ = p3.shape
    nc = t // C
    blk = lambda col, w: pl.BlockSpec((G, C, w), lambda i, c: (i, c, col // w))
    fixed = lambda shape: pl.BlockSpec(shape, lambda i, c: (0,) * len(shape))
    kern = functools.partial(_gdn_kernel, G=G, C=C, L=L, NC=nc)
    return pl.pallas_call(
        kern,
        grid=(b // G, nc),
        in_specs=[blk(COL_GQ, GDN_KEY), blk(COL_GK, GDN_KEY), blk(COL_GV, GDN_VAL), blk(COL_GZ, GDN_VAL),
                  blk(COL_SMALL, LANES),
                  fixed((GDN_CONV, GDN_CONV_DIM)), fixed((SUBLANES, LANES)), fixed((1, GDN_DV)),
                  pl.BlockSpec((None, G, GDN_CONV - 1, GDN_CONV_DIM), lambda i, c: (layer, i, 0, 0)),
                  pl.BlockSpec((None, G, GDN_HEADS, GDN_DK, GDN_DV), lambda i, c: (layer, i, 0, 0, 0))],
        out_specs=[pl.BlockSpec((G, C, GDN_VAL), lambda i, c: (i, c, 0)),
                   pl.BlockSpec((G, GDN_HEADS, GDN_DK, GDN_DV), lambda i, c: (i, 0, 0, 0)),
                   pl.BlockSpec((G, GDN_CONV - 1, GDN_CONV_DIM), lambda i, c: (i, 0, 0))],
        out_shape=[jax.ShapeDtypeStruct((b, t, GDN_VAL), F32),
                   jax.ShapeDtypeStruct((b, GDN_HEADS, GDN_DK, GDN_DV), F32),
                   jax.ShapeDtypeStruct((b, GDN_CONV - 1, GDN_CONV_DIM), F32)],
        scratch_shapes=[pltpu.VMEM((G, C + SUBLANES, GDN_CONV_DIM), F32),
                        pltpu.VMEM((G, GDN_HEADS, GDN_DK, GDN_DV), F32)],
        compiler_params=_cparams(("parallel", "arbitrary")),
        name="gdn",
    )(p3, p3, p3, p3, p3, conv_w, par, nw, conv_state, state)


def _ssd_kernel(x_ref, b_ref, c_ref, z_ref, sm_ref, cw_ref, cb_ref, par_ref, dexp_ref, nw_ref, e_ref,
                cs_ref, h0_ref, o_ref, ho_ref, co_ref, ext_ref, h_ref, *, G, C, L, NC):
    ci = pl.program_id(1)
    hist = SSM_CONV - 1
    nbc = SSM_GROUPS * SSM_DSTATE

    @pl.when(ci == 0)
    def _():
        h_ref[...] = h0_ref[...].reshape(h_ref.shape)
        ext_ref[...] = jnp.zeros(ext_ref.shape, F32)
        for g in range(G):
            ext_ref[g, SUBLANES - hist:SUBLANES, :] = cs_ref[g]

    causal, _ = _tri_masks(C)
    tril = jnp.where(causal, 1.0, 0.0).astype(BF16)
    triu = jnp.where(lax.broadcasted_iota(jnp.int32, (C, C), 0)
                     <= lax.broadcasted_iota(jnp.int32, (C, C), 1), 1.0, 0.0).astype(BF16)
    rows = lax.broadcasted_iota(jnp.int32, (C, 1), 0)
    valid = rows < L
    lane = lax.broadcasted_iota(jnp.int32, (1, LANES), 1)
    dt_lanes = (lane >= SM_DT) & (lane < SM_DT + SSM_HEADS)
    half = [jnp.where(lane < SSM_HEADDIM, 1.0, 0.0).astype(F32), jnp.where(lane >= SSM_HEADDIM, 1.0, 0.0).astype(F32)]
    dt_bias = par_ref[0:1, :]
    neg_a = -jnp.exp(par_ref[1:2, :])
    cw = cw_ref[...]
    cb = cb_ref[...]
    expand = e_ref[...]
    gw = SSM_INNER // SSM_GROUPS

    pre = []
    for g in range(G):
        ext_ref[g, SUBLANES:SUBLANES + C, 0:SSM_INNER] = x_ref[g]
        ext_ref[g, SUBLANES:SUBLANES + C, SSM_INNER:SSM_INNER + nbc] = b_ref[g]
        ext_ref[g, SUBLANES:SUBLANES + C, SSM_INNER + nbc:] = c_ref[g]
        xbc = _silu(_conv_from_ext(ext_ref, g, cw, C, SSM_CONV) + cb)
        new_hist = ext_ref[g, SUBLANES + L - hist:SUBLANES + L, :]
        co_ref[g] = new_hist
        ext_ref[g, SUBLANES - hist:SUBLANES, :] = new_hist

        dt = jnp.where(valid & dt_lanes, _softplus(sm_ref[g] + dt_bias), 0.0)
        la_exp = _mask_dot_rhs(_split3(dt * neg_a), expand)
        dt_exp = _mask_dot_rhs(_split3(dt), expand)
        la3 = _split3(la_exp)
        acum = _mask_dot(tril, la3)
        acum_t = _mask_dot_tn(la3, triu)
        x = xbc[:, :SSM_INNER]
        xdt = x * dt_exp
        pre.append(dict(xbc=xbc, x=x, xdt=xdt, acum=acum, acum_t=acum_t, cin=jnp.exp(acum),
                        xdec=(xdt * jnp.exp(acum[C - 1:C, :] - acum)).astype(BF16)))

    cbs = []
    for g in range(G):
        xbc = pre[g]["xbc"]
        for gi in range(SSM_GROUPS):
            bg = xbc[:, SSM_INNER + gi * SSM_DSTATE:SSM_INNER + (gi + 1) * SSM_DSTATE].astype(BF16)
            cg = xbc[:, SSM_INNER + nbc + gi * SSM_DSTATE:SSM_INNER + nbc + (gi + 1) * SSM_DSTATE].astype(BF16)
            cbs.append((bg, cg, _dot_nt(cg, bg)))

    ys, h_new = [], []
    for g in range(G):
        d = pre[g]
        for p in range(SSM_HEADS // 2):
            bg, cg, cbm = cbs[g * SSM_GROUPS + p // 2]
            ps = slice(p * LANES, (p + 1) * LANES)
            xdt_p = d["xdt"][:, ps]
            hp = h_ref[g, p]
            y_p = _dot_nt(cg, hp.astype(BF16)) * d["cin"][:, ps]
            for e in range(2):
                head = 2 * p + e
                acol = d["acum"][:, head * SSM_HEADDIM:head * SSM_HEADDIM + 1]
                arow = d["acum_t"][head * SSM_HEADDIM:head * SSM_HEADDIM + 1, :]
                seg = jnp.where(causal, jnp.exp(jnp.minimum(acol - arow, 0.0)), 0.0)
                y_p = y_p + _dot((cbm * seg).astype(BF16), (xdt_p * half[e]).astype(BF16))
            ys.append(y_p)
            h_new.append(hp * jnp.exp(d["acum_t"][p * LANES:(p + 1) * LANES, C - 1:C]) + _dot_tn(d["xdec"][:, ps], bg))

    outs = []
    for g in range(G):
        d = pre[g]
        y = jnp.concatenate(ys[g * (SSM_HEADS // 2):(g + 1) * (SSM_HEADS // 2)], axis=1)
        y = (y + dexp_ref[...] * d["x"]) * _silu(z_ref[g])
        outs.append(jnp.concatenate([_rms(y[:, gi * gw:(gi + 1) * gw], nw_ref[:, gi * gw:(gi + 1) * gw])
                                     for gi in range(SSM_GROUPS)], axis=1))
    o_ref[...] = jnp.stack(outs).astype(o_ref.dtype)
    h_ref[...] = jnp.stack(h_new).reshape(h_ref.shape)

    @pl.when(ci == NC - 1)
    def _():
        ho_ref[...] = h_ref[...].reshape(ho_ref.shape)


def _ssd(p3, conv_w, conv_b, par, dexp, nw, expand, conv_state, state, layer, *, G, C, L):
    b, t, _ = p3.shape
    nc = t // C
    nbc = SSM_GROUPS * SSM_DSTATE
    blk = lambda col, w: pl.BlockSpec((G, C, w), lambda i, c: (i, c, col // w))
    fixed = lambda shape: pl.BlockSpec(shape, lambda i, c: (0,) * len(shape))
    kern = functools.partial(_ssd_kernel, G=G, C=C, L=L, NC=nc)
    return pl.pallas_call(
        kern,
        grid=(b // G, nc),
        in_specs=[blk(COL_SX, SSM_INNER), blk(COL_SB, nbc), blk(COL_SC, nbc), blk(COL_SZ, SSM_INNER),
                  blk(COL_SMALL, LANES),
                  fixed((SSM_CONV, SSM_CONV_DIM)), fixed((1, SSM_CONV_DIM)), fixed((SUBLANES, LANES)),
                  fixed((1, SSM_INNER)), fixed((1, SSM_INNER)), fixed((LANES, SSM_INNER)),
                  pl.BlockSpec((None, G, SSM_CONV - 1, SSM_CONV_DIM), lambda i, c: (layer, i, 0, 0)),
                  pl.BlockSpec((None, G, SSM_HEADS, SSM_HEADDIM, SSM_DSTATE), lambda i, c: (layer, i, 0, 0, 0))],
        out_specs=[pl.BlockSpec((G, C, SSM_INNER), lambda i, c: (i, c, 0)),
                   pl.BlockSpec((G, SSM_HEADS, SSM_HEADDIM, SSM_DSTATE), lambda i, c: (i, 0, 0, 0)),
                   pl.BlockSpec((G, SSM_CONV - 1, SSM_CONV_DIM), lambda i, c: (i, 0, 0))],
        out_shape=[jax.ShapeDtypeStruct((b, t, SSM_INNER), F32),
                   jax.ShapeDtypeStruct((b, SSM_HEADS, SSM_HEADDIM, SSM_DSTATE), F32),
                   jax.ShapeDtypeStruct((b, SSM_CONV - 1, SSM_CONV_DIM), F32)],
        scratch_shapes=[pltpu.VMEM((G, C + SUBLANES, SSM_CONV_DIM), F32),
                        pltpu.VMEM((G, SSM_HEADS // 2, LANES, SSM_DSTATE), F32)],
        compiler_params=_cparams(("parallel", "arbitrary")),
        name="ssd",
    )(p3, p3, p3, p3, p3, conv_w, conv_b, par, dexp, nw, expand, conv_state, state)


def _split_dot(a, b_bf16):
    hi = a.astype(BF16)
    lo = (a - hi.astype(F32)).astype(BF16)
    return _dot(hi, b_bf16) + _dot(lo, b_bf16)


def _qknorm_kernel(q_ref, k_ref, v_ref, qw_ref, kw_ref, ones_ref, oq_ref, ok_ref, km_ref, qt_ref, kb_ref, vt_ref):
    ones = ones_ref[...]
    q = q_ref[...]
    k = k_ref[...]
    inv = 1.0 / ATTN_HD
    qn = q * lax.rsqrt(_split_dot(q * q, ones) * inv + EPS) * qw_ref[...]
    oq_ref[...] = qn
    kn = k * lax.rsqrt(_split_dot(k * k, ones) * inv + EPS) * kw_ref[...]
    ok_ref[...] = kn
    km_ref[0] = jnp.mean(kn, axis=0, keepdims=True)
    qt_ref[...] = (qn * (ATTN_HD ** -0.5)).T
    kb_ref[0] = kn.astype(BF16)
    vt_ref[0] = v_ref[...].T.astype(BF16)


def _qknorm(p2, qw, kw, ones, tm):
    m = p2.shape[0]
    nt = m // tm
    col = lambda c: pl.BlockSpec((tm, ATTN_WIDTH), lambda i: (i, c // ATTN_WIDTH))
    return pl.pallas_call(
        _qknorm_kernel,
        grid=(nt,),
        in_specs=[col(COL_AQ), col(COL_AK), col(COL_AV),
                  pl.BlockSpec((1, ATTN_WIDTH), lambda i: (0, 0)),
                  pl.BlockSpec((1, ATTN_WIDTH), lambda i: (0, 0)),
                  pl.BlockSpec((ATTN_WIDTH, ATTN_WIDTH), lambda i: (0, 0))],
        out_specs=[pl.BlockSpec((tm, ATTN_WIDTH), lambda i: (i, 0)),
                   pl.BlockSpec((tm, ATTN_WIDTH), lambda i: (i, 0)),
                   pl.BlockSpec((1, 1, ATTN_WIDTH), lambda i: (i, 0, 0)),
                   pl.BlockSpec((ATTN_WIDTH, tm), lambda i: (0, i)),
                   pl.BlockSpec((1, tm, ATTN_WIDTH), lambda i: (i, 0, 0)),
                   pl.BlockSpec((1, ATTN_WIDTH, tm), lambda i: (i, 0, 0))],
        out_shape=[jax.ShapeDtypeStruct((m, ATTN_WIDTH), F32),
                   jax.ShapeDtypeStruct((m, ATTN_WIDTH), F32),
                   jax.ShapeDtypeStruct((nt, 1, ATTN_WIDTH), F32),
                   jax.ShapeDtypeStruct((ATTN_WIDTH, m), F32),
                   jax.ShapeDtypeStruct((nt, tm, ATTN_WIDTH), BF16),
                   jax.ShapeDtypeStruct((nt, ATTN_WIDTH, tm), BF16)],
        compiler_params=_cparams(("parallel",)),
        name="qknorm",
    )(p2, p2, p2, qw, kw, ones)


def _top_blocks_neg(sc, n_valid):
    blk = lax.broadcasted_iota(jnp.int32, (sc.shape[0], 1), 0)
    sc = jnp.where(blk < n_valid, sc, NEG)
    sel = jnp.zeros(sc.shape, F32)
    for _ in range(MOBA_TOPK):
        mx = jnp.max(sc, axis=0, keepdims=True)
        idx = jnp.min(jnp.where(sc == mx, blk, sc.shape[0]), axis=0, keepdims=True)
        pick = blk == idx
        sel = jnp.where(pick, 1.0, sel)
        sc = jnp.where(pick, LOWEST, sc)
    return jnp.where((blk < n_valid) & (sel > 0.0), 0.0, NEG)


def _moba_kernel(qt_ref, kb_ref, vt_ref, km_ref, bias_ref, o_ref,
                 qm_ref, sel_ref, m_ref, l_ref, acc_ref, lg_ref):
    jq = pl.program_id(1)
    row = lax.broadcasted_iota(jnp.int32, (LANES, 1), 0)
    half = [row < ATTN_HD, row >= ATTN_HD]

    qt = qt_ref[...]
    km = km_ref[0]
    for h in range(ATTN_HEADS):
        ps = slice((h // 2) * LANES, (h // 2 + 1) * LANES)
        qm = jnp.where(half[h % 2], qt[ps, :], 0.0)
        qm_ref[h] = qm.astype(BF16)
        sel_ref[h] = _top_blocks_neg(_dot3(_split2(km[:, ps]), _split2(qm)), jq)
    m_ref[...] = jnp.full(m_ref.shape, NEG, F32)
    l_ref[...] = jnp.zeros(l_ref.shape, F32)
    acc_ref[...] = jnp.zeros(acc_ref.shape, F32)
    ones = jnp.ones((2 * SUBLANES, MOBA_BLOCK), BF16)

    def key_block(n, carry):
        d = jq - n
        dd = jnp.minimum(d, BIAS_ND - 1)
        kb = kb_ref[n]
        vt = vt_ref[n]
        m_old = m_ref[...]
        l_old = l_ref[...]
        m_new = []
        for h in range(ATTN_HEADS):
            ps = slice((h // 2) * LANES, (h // 2 + 1) * LANES)
            rowneg = jnp.where(d == 0, 0.0, sel_ref[h, pl.ds(n, 1), :])
            lg = _dot(kb[:, ps], qm_ref[h]) + bias_ref[h, dd] + rowneg
            lg_ref[h] = lg
            m_new.append(jnp.maximum(m_old[h:h + 1, :], jnp.max(lg, axis=0, keepdims=True)))
        l_new, acc_new = [], []
        for h in range(ATTN_HEADS):
            alpha = jnp.exp(m_old[h:h + 1, :] - m_new[h])
            pt = jnp.exp(lg_ref[h] - m_new[h]).astype(BF16)
            lhs = jnp.concatenate([vt[h * ATTN_HD:(h + 1) * ATTN_HD, :], ones], axis=0)
            r = _dot(lhs, pt)
            l_new.append(l_old[h:h + 1, :] * alpha + r[ATTN_HD:ATTN_HD + 1, :])
            acc_new.append(acc_ref[h] * alpha + r[0:ATTN_HD, :])
        m_ref[...] = jnp.concatenate(m_new, axis=0)
        l_ref[...] = jnp.concatenate(l_new, axis=0)
        acc_ref[...] = jnp.stack(acc_new)
        return carry

    lax.fori_loop(0, jq + 1, key_block, 0)
    l = l_ref[...]
    ot = jnp.concatenate([acc_ref[h] / l[h:h + 1, :] for h in range(ATTN_HEADS)], axis=0)
    o_ref[0] = ot.T.astype(o_ref.dtype)


def _moba_prompt(qt, kb, vt, kmean, bias_tiles, b):
    nb = kb.shape[0] // b
    t = nb * MOBA_BLOCK
    once = pl.Buffered(1)
    return pl.pallas_call(
        _moba_kernel,
        grid=(b, nb),
        in_specs=[pl.BlockSpec((ATTN_WIDTH, MOBA_BLOCK), lambda i, j: (0, i * nb + j)),
                  pl.BlockSpec((nb, MOBA_BLOCK, ATTN_WIDTH), lambda i, j: (i, 0, 0), pipeline_mode=once),
                  pl.BlockSpec((nb, ATTN_WIDTH, MOBA_BLOCK), lambda i, j: (i, 0, 0), pipeline_mode=once),
                  pl.BlockSpec((1, LANES, ATTN_WIDTH), lambda i, j: (i, 0, 0)),
                  pl.BlockSpec((ATTN_HEADS, BIAS_ND, MOBA_BLOCK, MOBA_BLOCK), lambda i, j: (0, 0, 0, 0),
                               pipeline_mode=once)],
        out_specs=pl.BlockSpec((1, MOBA_BLOCK, ATTN_WIDTH), lambda i, j: (i, j, 0)),
        out_shape=jax.ShapeDtypeStruct((b, t, ATTN_WIDTH), F32),
        scratch_shapes=[pltpu.VMEM((ATTN_HEADS, LANES, MOBA_BLOCK), BF16),
                        pltpu.VMEM((ATTN_HEADS, LANES, MOBA_BLOCK), F32),
                        pltpu.VMEM((ATTN_HEADS, MOBA_BLOCK), F32),
                        pltpu.VMEM((ATTN_HEADS, MOBA_BLOCK), F32),
                        pltpu.VMEM((ATTN_HEADS, ATTN_HD, MOBA_BLOCK), F32),
                        pltpu.VMEM((ATTN_HEADS, MOBA_BLOCK, MOBA_BLOCK), F32)],
        compiler_params=_cparams(("parallel", "arbitrary")),
        name="moba_prompt",
    )(qt, kb, vt, kmean, bias_tiles)


def _moba_sample_kernel(pt_ref, q_ref, kn_ref, vn_ref, bias_ref, bown_ref, hm_ref, pick_ref, ck_hbm, cv_hbm,
                        o_ref, kbuf, vbuf, sem, *, layer, npages, page, nseq, nq):
    b = pl.program_id(0)
    slot = lax.rem(b, 2)

    def page_copies(seq, sl):
        out = []
        for p in range(npages):
            pg = pt_ref[seq, p]
            out.append(pltpu.make_async_copy(ck_hbm.at[layer, pg], kbuf.at[sl, pl.ds(p * page, page)], sem.at[0, sl]))
            out.append(pltpu.make_async_copy(cv_hbm.at[layer, pg], vbuf.at[sl, pl.ds(p * page, page)], sem.at[1, sl]))
        return out

    @pl.when(b == 0)
    def _():
        for cp in page_copies(0, 0):
            cp.start()

    @pl.when(b + 1 < nseq)
    def _():
        for cp in page_copies(b + 1, 1 - slot):
            cp.start()

    for cp in page_copies(b, slot):
        cp.wait()

    past = npages * page
    nblk = past // MOBA_BLOCK
    nrep = ATTN_HEADS * nq
    hm = hm_ref[...]
    q = q_ref[0] * (ATTN_HD ** -0.5)
    qrep = _dot_tn(pick_ref[...], q, HI) * hm
    kp = kbuf[slot]
    vp = vbuf[slot]
    kmean = jnp.mean(kp.reshape(nblk, MOBA_BLOCK, ATTN_WIDTH), axis=1)
    sc = _dot_nt(kmean, qrep, HI)
    blk = lax.broadcasted_iota(jnp.int32, (nblk, 1), 0)
    sel = jnp.zeros(sc.shape, F32)
    for _ in range(min(MOBA_TOPK, nblk)):
        mx = jnp.max(sc, axis=0, keepdims=True)
        idx = jnp.min(jnp.where(sc == mx, blk, nblk), axis=0, keepdims=True)
        pick = blk == idx
        sel = jnp.where(pick, 1.0, sel)
        sc = jnp.where(pick, LOWEST, sc)
    selk = jnp.broadcast_to(sel[:, None, :], (nblk, MOBA_BLOCK, nrep)).reshape(past, nrep) > 0.0

    qb = qrep.astype(BF16)
    lg = jnp.where(selk, _dot_nt(kp.astype(BF16), qb) + bias_ref[...], NEG)
    lo = _dot_nt(kn_ref[0].astype(BF16), qb) + bown_ref[...]
    m = jnp.maximum(jnp.max(lg, axis=0, keepdims=True), jnp.max(lo, axis=0, keepdims=True))
    pp = jnp.where(selk, jnp.exp(lg - m), 0.0)
    po = jnp.exp(lo - m)
    ones = jnp.ones((past, LANES), F32)
    den = _dot_tn(pp, ones, HI) + _dot_tn(po, ones[0:SUBLANES], HI)
    r = _dot_tn(pp.astype(BF16), vp.astype(BF16)) + _dot_tn(po.astype(BF16), vn_ref[0].astype(BF16))
    r = r / den[:, 0:1] * hm
    o_ref[0] = _dot(pick_ref[...], r, HI).astype(o_ref.dtype)


def _moba_sample(page_table, aq, ak, av_p3, bias_t, bias_own, head_mask, pick, cache_k, cache_v, layer, nq):
    nseq = aq.shape[0]
    npages = page_table.shape[1]
    page = cache_k.shape[2]
    past = npages * page
    nrep = ATTN_HEADS * nq
    kern = functools.partial(_moba_sample_kernel, layer=layer, npages=npages, page=page, nseq=nseq, nq=nq)
    gs = pltpu.PrefetchScalarGridSpec(
        num_scalar_prefetch=1,
        grid=(nseq,),
        in_specs=[pl.BlockSpec((1, SUBLANES, ATTN_WIDTH), lambda i, pt: (i, 0, 0)),
                  pl.BlockSpec((1, SUBLANES, ATTN_WIDTH), lambda i, pt: (i, 0, 0)),
                  pl.BlockSpec((1, SUBLANES, ATTN_WIDTH), lambda i, pt: (i, 0, COL_AV // ATTN_WIDTH)),
                  pl.BlockSpec((past, nrep), lambda i, pt: (0, 0)),
                  pl.BlockSpec((SUBLANES, nrep), lambda i, pt: (0, 0)),
                  pl.BlockSpec((nrep, ATTN_WIDTH), lambda i, pt: (0, 0)),
                  pl.BlockSpec((SUBLANES, nrep), lambda i, pt: (0, 0)),
                  pl.BlockSpec(memory_space=pl.ANY),
                  pl.BlockSpec(memory_space=pl.ANY)],
        out_specs=pl.BlockSpec((1, SUBLANES, ATTN_WIDTH), lambda i, pt: (i, 0, 0)),
        scratch_shapes=[pltpu.VMEM((2, past, ATTN_WIDTH), F32),
                        pltpu.VMEM((2, past, ATTN_WIDTH), F32),
                        pltpu.SemaphoreType.DMA((2, 2))])
    return pl.pallas_call(
        kern,
        grid_spec=gs,
        out_shape=jax.ShapeDtypeStruct((nseq, SUBLANES, ATTN_WIDTH), F32),
        compiler_params=_cparams(("arbitrary",)),
        name="moba_sample",
    )(page_table, aq, ak, av_p3, bias_t, bias_own, head_mask, pick, cache_k, cache_v)


def _merge_kernel(x_ref, og_ref, oa_ref, os_ref, g0_ref, g1_ref, g2_ref, wg_ref, wa_ref, ws_ref, wo_ref, o_ref):
    merged = (_sigmoid(g0_ref[...]) * _dot(og_ref[...].astype(BF16), wg_ref[...])
              + _sigmoid(g1_ref[...]) * _dot(oa_ref[...].astype(BF16), wa_ref[...])
              + _sigmoid(g2_ref[...]) * _dot(os_ref[...].astype(BF16), ws_ref[...]))
    o_ref[...] = x_ref[...] + _dot(merged.astype(BF16), wo_ref[...])


def _merge(x, o_gdn, o_att, o_ssm, p2, wg, wa, ws, wo, tm):
    m = x.shape[0]
    row = lambda w, cb=0: pl.BlockSpec((tm, w), lambda i: (i, cb))
    fixed = lambda a, bb: pl.BlockSpec((a, bb), lambda i: (0, 0))
    return pl.pallas_call(
        _merge_kernel,
        grid=(m // tm,),
        in_specs=[row(D_MODEL), row(GDN_VAL), row(ATTN_WIDTH), row(SSM_INNER),
                  row(D_MODEL, COL_G0 // D_MODEL), row(D_MODEL, COL_G1 // D_MODEL), row(D_MODEL, COL_G2 // D_MODEL),
                  fixed(GDN_VAL, D_MODEL), fixed(ATTN_WIDTH, D_MODEL), fixed(SSM_INNER, D_MODEL),
                  fixed(D_MODEL, D_MODEL)],
        out_specs=row(D_MODEL),
        out_shape=jax.ShapeDtypeStruct((m, D_MODEL), F32),
        compiler_params=_cparams(("parallel",)),
        name="merge",
    )(x, o_gdn, o_att, o_ssm, p2, p2, p2, wg, wa, ws, wo)


FFN_HALO = 16


def _ffn_prompt_kernel(x_ref, xh_ref, nw_ref, upg_ref, upv_ref, cwg_ref, cwv_ref, cbg_ref, cbv_ref, dn_ref,
                       o_ref, tg_ref, tv_ref, h_ref, ug_ref, uv_ref, *, tm, seq_tiles):
    i = pl.program_id(0)
    j = pl.program_id(1)

    @pl.when(j == 0)
    def _():
        x = x_ref[...]
        h_ref[FFN_HALO:, :] = _rms(x, nw_ref[...]).astype(BF16)
        h_ref[0:FFN_HALO, :] = _rms(xh_ref[...], nw_ref[...]).astype(BF16)
        o_ref[...] = x

    first = lax.rem(i, seq_tiles) == 0
    rows = lax.broadcasted_iota(jnp.int32, (tm + FFN_HALO, 1), 0)
    keep = jnp.where(first & (rows < FFN_HALO), 0.0, 1.0)
    h = h_ref[...]
    ug_ref[...] = _dot(h, upg_ref[...]) * keep
    uv_ref[...] = _dot(h, upv_ref[...]) * keep

    def conv(u_ref, cw_ref, cb_ref):
        acc = cb_ref[...]
        for k in range(FFN_CONV):
            off = FFN_HALO - (FFN_CONV - 1) + k
            acc = acc + cw_ref[k:k + 1, :] * u_ref[off:off + tm, :]
        return acc

    act = _silu(conv(ug_ref, cwg_ref, cbg_ref)) * conv(uv_ref, cwv_ref, cbv_ref)
    o_ref[...] += _dot(act.astype(BF16), dn_ref[...])
    tg_ref[0] = ug_ref[tm + FFN_HALO - SUBLANES:, :]
    tv_ref[0] = uv_ref[tm + FFN_HALO - SUBLANES:, :]


def _ffn_prompt(x, nw, up, cw, cb, dn, tm, seq_len):
    m = x.shape[0]
    nt = m // tm
    nj = D_FF // FFN_TN
    hb = tm // FFN_HALO
    kern = functools.partial(_ffn_prompt_kernel, tm=tm, seq_tiles=seq_len // tm)
    vec = lambda r, off: pl.BlockSpec((r, FFN_TN), lambda i, j: (0, j + off))
    return pl.pallas_call(
        kern,
        grid=(nt, nj),
        in_specs=[pl.BlockSpec((tm, D_MODEL), lambda i, j: (i, 0)),
                  pl.BlockSpec((FFN_HALO, D_MODEL), lambda i, j: (jnp.maximum(i * hb - 1, 0), 0)),
                  pl.BlockSpec((1, D_MODEL), lambda i, j: (0, 0)),
                  pl.BlockSpec((D_MODEL, FFN_TN), lambda i, j: (0, j)),
                  pl.BlockSpec((D_MODEL, FFN_TN), lambda i, j: (0, j + nj)),
                  vec(FFN_CONV, 0), vec(FFN_CONV, nj), vec(1, 0), vec(1, nj),
                  pl.BlockSpec((FFN_TN, D_MODEL), lambda i, j: (j, 0))],
        out_specs=[pl.BlockSpec((tm, D_MODEL), lambda i, j: (i, 0)),
                   pl.BlockSpec((1, SUBLANES, FFN_TN), lambda i, j: (i, 0, j)),
                   pl.BlockSpec((1, SUBLANES, FFN_TN), lambda i, j: (i, 0, j))],
        out_shape=[jax.ShapeDtypeStruct((m, D_MODEL), F32),
                   jax.ShapeDtypeStruct((nt, SUBLANES, D_FF), F32),
                   jax.ShapeDtypeStruct((nt, SUBLANES, D_FF), F32)],
        scratch_shapes=[pltpu.VMEM((tm + FFN_HALO, D_MODEL), BF16),
                        pltpu.VMEM((tm + FFN_HALO, FFN_TN), F32),
                        pltpu.VMEM((tm + FFN_HALO, FFN_TN), F32)],
        compiler_params=_cparams(("parallel", "arbitrary")),
        name="ffn_prompt",
    )(x, x, nw, up, up, cw, cw, cb, cb, dn)


def _ffn_sample_kernel(x_ref, nw_ref, upg_ref, upv_ref, cwg_ref, cwv_ref, cbg_ref, cbv_ref, dn_ref, sg_ref, sv_ref,
                       o_ref, ng_ref, nv_ref, h_ref, *, nb):
    j = pl.program_id(0)

    @pl.when(j == 0)
    def _():
        x = x_ref[...]
        h_ref[...] = _rms(x, nw_ref[...]).astype(BF16)
        o_ref[...] = x

    h = h_ref[...]
    hist = FFN_CONV - 1

    def conv(u, s_ref, cw_ref, cb_ref, n_ref):
        full = jnp.concatenate([s_ref[...], u], axis=0)
        rows = u.shape[0]
        acc = cb_ref[...]
        for k in range(FFN_CONV):
            acc = acc + cw_ref[k:k + 1, :] * full[k * nb:k * nb + rows, :]
        n_ref[...] = full[rows:, :]
        return acc

    cg = conv(_dot(h, upg_ref[...]), sg_ref, cwg_ref, cbg_ref, ng_ref)
    cv = conv(_dot(h, upv_ref[...]), sv_ref, cwv_ref, cbv_ref, nv_ref)
    o_ref[...] += _dot((_silu(cg) * cv).astype(BF16), dn_ref[...])


def _ffn_sample(x, nw, up, cw, cb, dn, state_g, state_v, nb):
    m = x.shape[0]
    nj = D_FF // FFN_TN
    srows = (FFN_CONV - 1) * nb
    kern = functools.partial(_ffn_sample_kernel, nb=nb)
    vec = lambda r, off: pl.BlockSpec((r, FFN_TN), lambda j: (0, j + off))
    return pl.pallas_call(
        kern,
        grid=(nj,),
        in_specs=[pl.BlockSpec((m, D_MODEL), lambda j: (0, 0)),
                  pl.BlockSpec((1, D_MODEL), lambda j: (0, 0)),
                  pl.BlockSpec((D_MODEL, FFN_TN), lambda j: (0, j)),
                  pl.BlockSpec((D_MODEL, FFN_TN), lambda j: (0, j + nj)),
                  vec(FFN_CONV, 0), vec(FFN_CONV, nj), vec(1, 0), vec(1, nj),
                  pl.BlockSpec((FFN_TN, D_MODEL), lambda j: (j, 0)),
                  pl.BlockSpec((srows, FFN_TN), lambda j: (0, j)),
                  pl.BlockSpec((srows, FFN_TN), lambda j: (0, j))],
        out_specs=[pl.BlockSpec((m, D_MODEL), lambda j: (0, 0)),
                   pl.BlockSpec((srows, FFN_TN), lambda j: (0, j)),
                   pl.BlockSpec((srows, FFN_TN), lambda j: (0, j))],
        out_shape=[jax.ShapeDtypeStruct((m, D_MODEL), F32),
                   jax.ShapeDtypeStruct((srows, D_FF), F32),
                   jax.ShapeDtypeStruct((srows, D_FF), F32)],
        scratch_shapes=[pltpu.VMEM((m, D_MODEL), BF16)],
        compiler_params=_cparams(("arbitrary",)),
        name="ffn_sample",
    )(x, nw, up, up, cw, cw, cb, cb, dn, state_g, state_v)


def _t5_bucket(rel):
    n = jnp.maximum(rel, 0)
    exact = NUM_BUCKETS // 2
    nf = jnp.maximum(n, 1).astype(F32)
    large = exact + (jnp.log(nf / exact) / math.log(MAX_DISTANCE / exact) * (NUM_BUCKETS - exact)).astype(jnp.int32)
    return jnp.where(n < exact, n, jnp.minimum(large, NUM_BUCKETS - 1))


def _bias_by_distance(rel_bias, max_rel):
    return rel_bias.astype(F32)[_t5_bucket(jnp.arange(max_rel)), :].T


def _prompt_bias_tiles(rel_bias):
    blk = MOBA_BLOCK
    tab = _bias_by_distance(rel_bias, (BIAS_ND + 1) * blk)
    tab = jnp.concatenate([jnp.zeros((ATTN_HEADS, blk), F32), tab], axis=1)
    causal = np.arange(blk)[:, None] <= np.arange(blk)[None, :]
    tiles = []
    for d in range(BIAS_ND):
        w = tab[:, d * blk + 1:(d + 2) * blk]
        w = jnp.concatenate([w, jnp.zeros((ATTN_HEADS, 1), F32)], axis=1)
        rows = jnp.tile(w, (1, blk))[:, :blk * (2 * blk - 1)].reshape(ATTN_HEADS, blk, 2 * blk - 1)
        tile = rows[:, :, blk - 1:]
        tiles.append(jnp.where(causal, tile, NEG) if d == 0 else tile)
    return jnp.stack(tiles, axis=1)


def _sample_bias_tables(rel_bias, past_len, nq):
    tab = _bias_by_distance(rel_bias, past_len + nq)
    kpos = np.arange(past_len)[:, None]
    rel = past_len + np.arange(nq)[None, :] - kpos
    past_t = jnp.transpose(tab[:, rel], (1, 0, 2)).reshape(past_len, ATTN_HEADS * nq)
    ko = np.arange(SUBLANES)[:, None]
    relo = np.arange(nq)[None, :] - ko
    own = jnp.transpose(tab[:, np.maximum(relo, 0)], (1, 0, 2))
    ok = jnp.asarray(((relo >= 0) & (ko < nq))[:, None, :])
    own_t = jnp.where(ok, own, NEG).reshape(SUBLANES, ATTN_HEADS * nq)
    return past_t, own_t


def _permute_w_in(w):
    cuts = np.cumsum((0,) + IN_SIZES)
    sec = lambda k, lo=0, hi=None: w[:, cuts[k] + lo: (cuts[k + 1] if hi is None else cuts[k] + hi)]
    nbc = SSM_GROUPS * SSM_DSTATE
    small = jnp.concatenate([sec(2), sec(3), sec(9),
                             jnp.zeros((w.shape[0], LANES - 2 * GDN_HEADS - SSM_HEADS), w.dtype)], axis=1)
    parts = [sec(7), sec(8, 0, SSM_INNER),
             sec(10, 0, D_MODEL), sec(10, D_MODEL, 2 * D_MODEL), sec(10, 2 * D_MODEL, 3 * D_MODEL),
             sec(0, 0, GDN_KEY), sec(0, GDN_KEY, 2 * GDN_KEY), sec(0, 2 * GDN_KEY, GDN_CONV_DIM), sec(1),
             sec(4), sec(5), sec(6),
             sec(8, SSM_INNER, SSM_INNER + nbc), sec(8, SSM_INNER + nbc, SSM_CONV_DIM), small]
    return jnp.concatenate(parts, axis=1).astype(BF16)


def _lane_row(vals, offset):
    return jnp.zeros((LANES,), F32).at[offset:offset + vals.shape[0]].set(vals.astype(F32))


def _pick_tm(m, cap):
    tm = cap
    while m % tm:
        tm //= 2
    return tm


def kernel(x_prompt, x_sample, cache_k, cache_v, page_table, state_gdn, state_gdn_conv, state_ssm, state_ssm_conv, state_ffn_conv, rel_bias, norm_mix, w_in, gdn_conv_w, gdn_A_log, gdn_dt_bias, gdn_norm, q_norm, k_norm, ssm_conv_w, ssm_conv_b, ssm_A_log, ssm_dt_bias, ssm_D, ssm_norm, w_br_gdn, w_br_attn, w_br_ssm, w_out, norm_ffn, ffn_up, ffn_conv_w, ffn_conv_b, ffn_down):
    depth = w_in.shape[0]
    bp, seq, _ = x_prompt.shape
    bd, nq, _ = x_sample.shape
    npool, page = cache_k.shape[1], cache_k.shape[2]
    npages = page_table.shape[1]
    past_len = npages * page
    nblk = seq // MOBA_BLOCK
    assert seq % MOBA_BLOCK == 0 and nblk <= LANES and past_len % MOBA_BLOCK == 0
    assert nq <= SUBLANES and bd % SUBLANES == 0

    ck = cache_k.reshape(depth, npool, page, ATTN_WIDTH)
    cv = cache_v.reshape(depth, npool, page, ATTN_WIDTH)
    bias_tiles = _prompt_bias_tiles(rel_bias)
    bias_past, bias_own = _sample_bias_tables(rel_bias, past_len, nq)
    nrep = ATTN_HEADS * nq
    col_head = np.arange(nrep) // nq
    head_mask = jnp.asarray((np.arange(ATTN_WIDTH)[None, :] // ATTN_HD == col_head[:, None]).astype(np.float32))
    pick = jnp.asarray((np.arange(SUBLANES)[:, None] == (np.arange(nrep) % nq)[None, :]).astype(np.float32))
    ones_bd = jnp.asarray((np.arange(ATTN_WIDTH)[:, None] // ATTN_HD
                           == np.arange(ATTN_WIDTH)[None, :] // ATTN_HD).astype(np.float32)).astype(BF16)
    expand = jnp.asarray((np.arange(LANES)[:, None] - SM_DT
                          == np.arange(SSM_INNER)[None, :] // SSM_HEADDIM).astype(np.float32)).astype(BF16)
    zeros_gdn_state = jnp.zeros((1, bp, GDN_HEADS, GDN_DK, GDN_DV), F32)
    zeros_gdn_conv = jnp.zeros((1, bp, GDN_CONV - 1, GDN_CONV_DIM), F32)
    zeros_ssm_state = jnp.zeros((1, bp, SSM_HEADS, SSM_HEADDIM, SSM_DSTATE), F32)
    zeros_ssm_conv = jnp.zeros((1, bp, SSM_CONV - 1, SSM_CONV_DIM), F32)

    mp = bp * seq
    tm_p = _pick_tm(mp, 1024)
    tm_ffn = _pick_tm(seq, 512)
    chunk = math.gcd(seq, 64)
    ms = bd * SUBLANES
    yp = x_prompt.reshape(mp, D_MODEL)
    ys = x_sample
    out_p = [[] for _ in range(7)]
    out_s = [[] for _ in range(7)]

    for l in range(depth):
        w_perm = _permute_w_in(w_in[l])
        nmix = norm_mix[l].reshape(1, D_MODEL)
        gdn_par = jnp.zeros((SUBLANES, LANES), F32).at[0].set(_lane_row(gdn_dt_bias[l], SM_A)).at[1].set(
            _lane_row(gdn_A_log[l], SM_A))
        ssm_par = jnp.zeros((SUBLANES, LANES), F32).at[0].set(_lane_row(ssm_dt_bias[l], SM_DT)).at[1].set(
            _lane_row(ssm_A_log[l], SM_DT))
        gnw = gdn_norm[l].reshape(1, GDN_DV)
        qw = jnp.tile(q_norm[l], ATTN_HEADS).reshape(1, ATTN_WIDTH)
        kw = jnp.tile(k_norm[l], ATTN_HEADS).reshape(1, ATTN_WIDTH)
        scb = ssm_conv_b[l].reshape(1, SSM_CONV_DIM)
        dexp = jnp.repeat(ssm_D[l], SSM_HEADDIM).reshape(1, SSM_INNER)
        snw = ssm_norm[l].reshape(1, SSM_INNER)
        wg, wa, ws, wo = (w_br_gdn[l].astype(BF16), w_br_attn[l].astype(BF16), w_br_ssm[l].astype(BF16),
                          w_out[l].astype(BF16))
        nffn = norm_ffn[l].reshape(1, D_MODEL)
        up = ffn_up[l].astype(BF16)
        dn = ffn_down[l].astype(BF16)
        fcw = ffn_conv_w[l]
        fcb = ffn_conv_b[l].reshape(1, 2 * D_FF)

        p2 = _norm_proj(yp, nmix, w_perm, tm_p)
        p3 = p2.reshape(bp, seq, P_WIDTH)
        og, gs_new, gc_new = _gdn(p3, gdn_conv_w[l], gdn_par, gnw, zeros_gdn_conv, zeros_gdn_state, 0,
                                  G=bp, C=chunk, L=chunk)
        osm, hs_new, sc_new = _ssd(p3, ssm_conv_w[l], scb, ssm_par, dexp, snw, expand, zeros_ssm_conv,
                                   zeros_ssm_state, 0, G=bp, C=chunk, L=chunk)
        _, ak, km, qt, kb, vt = _qknorm(p2, qw, kw, ones_bd, MOBA_BLOCK)
        kmean = jnp.pad(km.reshape(bp, nblk, ATTN_WIDTH), ((0, 0), (0, LANES - nblk), (0, 0)))
        oa = _moba_prompt(qt, kb, vt, kmean, bias_tiles, bp)
        x1 = _merge(yp, og.reshape(mp, GDN_VAL), oa.reshape(mp, ATTN_WIDTH), osm.reshape(mp, SSM_INNER), p2,
                    wg, wa, ws, wo, _pick_tm(mp, 512))
        yp, tg, tv = _ffn_prompt(x1, nffn, up, fcw, fcb, dn, tm_ffn, seq)
        tpb = seq // tm_ffn
        last = np.arange(bp) * tpb + tpb - 1
        ffn_state = jnp.concatenate([tg[last], tv[last]], axis=-1)[:, SUBLANES - (FFN_CONV - 1):, :]
        for acc, tns in zip(out_p, (ak.reshape(bp, seq, ATTN_HEADS, ATTN_HD),
                                    p3[:, :, COL_AV:COL_AV + ATTN_WIDTH].reshape(bp, seq, ATTN_HEADS, ATTN_HD),
                                    gs_new, gc_new, hs_new, sc_new, ffn_state)):
            acc.append(tns)

        xs_pad = jnp.pad(ys, ((0, 0), (0, SUBLANES - nq), (0, 0))).reshape(ms, D_MODEL)
        q2 = _norm_proj(xs_pad, nmix, w_perm, _pick_tm(ms, 1024))
        q3 = q2.reshape(bd, SUBLANES, P_WIDTH)
        gsz = 4
        og_s, gs_s, gc_s = _gdn(q3, gdn_conv_w[l], gdn_par, gnw, state_gdn_conv, state_gdn, l,
                                G=gsz, C=SUBLANES, L=nq)
        os_s, hs_s, sc_s = _ssd(q3, ssm_conv_w[l], scb, ssm_par, dexp, snw, expand, state_ssm_conv, state_ssm, l,
                                G=gsz, C=SUBLANES, L=nq)
        aq_s, ak_s, _, _, _, _ = _qknorm(q2, qw, kw, ones_bd, _pick_tm(ms, MOBA_BLOCK))
        oa_s = _moba_sample(page_table, aq_s.reshape(bd, SUBLANES, ATTN_WIDTH), ak_s.reshape(bd, SUBLANES, ATTN_WIDTH),
                            q3, bias_past, bias_own, head_mask, pick, ck, cv, l, nq)
        x1_s = _merge(xs_pad, og_s.reshape(ms, GDN_VAL), oa_s.reshape(ms, ATTN_WIDTH), os_s.reshape(ms, SSM_INNER), q2,
                      wg, wa, ws, wo, _pick_tm(ms, 512))
        x1_t = jnp.transpose(x1_s.reshape(bd, SUBLANES, D_MODEL)[:, :nq], (1, 0, 2)).reshape(nq * bd, D_MODEL)
        st = jnp.transpose(state_ffn_conv[l], (1, 0, 2)).reshape((FFN_CONV - 1) * bd, 2 * D_FF)
        y_t, ng, nv = _ffn_sample(x1_t, nffn, up, fcw, fcb, dn, st[:, :D_FF], st[:, D_FF:], bd)
        ys = jnp.transpose(y_t.reshape(nq, bd, D_MODEL), (1, 0, 2))
        ffn_state_s = jnp.transpose(jnp.concatenate([ng, nv], axis=-1).reshape(FFN_CONV - 1, bd, 2 * D_FF), (1, 0, 2))
        for acc, tns in zip(out_s, (ak_s.reshape(bd, SUBLANES, ATTN_HEADS, ATTN_HD)[:, :nq],
                                    q3[:, :nq, COL_AV:COL_AV + ATTN_WIDTH].reshape(bd, nq, ATTN_HEADS, ATTN_HD),
                                    gs_s, gc_s, hs_s, sc_s, ffn_state_s)):
            acc.append(tns)

    k_p, v_p, gdn_p, gdn_conv_p, ssm_p, ssm_conv_p, ffn_conv_p = [jnp.stack(a) for a in out_p]
    k_s, v_s, gdn_s, gdn_conv_s, ssm_s, ssm_conv_s, ffn_conv_s = [jnp.stack(a) for a in out_s]
    return (yp.reshape(bp, seq, D_MODEL), ys, k_p, v_p, gdn_p, gdn_conv_p, ssm_p, ssm_conv_p, ffn_conv_p,
            k_s, v_s, gdn_s, gdn_conv_s, ssm_s, ssm_conv_s, ffn_conv_s)
```

```python
import functools
import math

import numpy as np
import jax
import jax.numpy as jnp
from jax import lax
from jax.experimental import pallas as pl
from jax.experimental.pallas import tpu as pltpu

F32 = jnp.float32
BF16 = jnp.bfloat16
HI = lax.Precision.HIGHEST

D_MODEL = 1024
GDN_HEADS, GDN_DK, GDN_DV, GDN_CONV = 4, 128, 128, 4
GDN_KEY = GDN_HEADS * GDN_DK
GDN_VAL = GDN_HEADS * GDN_DV
GDN_CONV_DIM = 2 * GDN_KEY + GDN_VAL
ATTN_HEADS, ATTN_HD = 8, 64
ATTN_WIDTH = ATTN_HEADS * ATTN_HD
MOBA_BLOCK, MOBA_TOPK = 256, 3
NUM_BUCKETS, MAX_DISTANCE = 32, 1024
SSM_INNER, SSM_HEADDIM, SSM_GROUPS, SSM_DSTATE, SSM_CONV = 1024, 64, 4, 128, 4
SSM_HEADS = SSM_INNER // SSM_HEADDIM
SSM_CONV_DIM = SSM_INNER + 2 * SSM_GROUPS * SSM_DSTATE
N_BRANCH = 3
D_FF = 11 * D_MODEL // 4
FFN_CONV = 3
IN_SIZES = (GDN_CONV_DIM, GDN_VAL, GDN_HEADS, GDN_HEADS, ATTN_WIDTH, ATTN_WIDTH, ATTN_WIDTH,
            SSM_INNER, SSM_CONV_DIM, SSM_HEADS, N_BRANCH * D_MODEL)
EPS = 1e-6
NEG = -1e30
LOWEST = -3e38

LANES = 128
SUBLANES = 8
VMEM_LIMIT = 56 * 1024 * 1024

COL_SZ, COL_SX, COL_G0, COL_G1, COL_G2 = 0, 1024, 2048, 3072, 4096
COL_GQ, COL_GK, COL_GV, COL_GZ = 5120, 5632, 6144, 6656
COL_AQ, COL_AK, COL_AV = 7168, 7680, 8192
COL_SB, COL_SC = 8704, 9216
COL_SMALL = 9728
P_WIDTH = COL_SMALL + LANES
SM_A, SM_B, SM_DT = 0, GDN_HEADS, 2 * GDN_HEADS
FFN_TN = 1408
BIAS_ND = 6

def _cparams(sem):
    return pltpu.CompilerParams(dimension_semantics=sem, vmem_limit_bytes=VMEM_LIMIT)


def _dot(a, b, prec=None):
    return lax.dot_general(a, b, (((1,), (0,)), ((), ())), precision=prec, preferred_element_type=F32)


def _dot_nt(a, b, prec=None):
    return lax.dot_general(a, b, (((1,), (1,)), ((), ())), precision=prec, preferred_element_type=F32)


def _dot_tn(a, b, prec=None):
    return lax.dot_general(a, b, (((0,), (0,)), ((), ())), precision=prec, preferred_element_type=F32)


def _sigmoid(x):
    return 1.0 / (1.0 + jnp.exp(-x))


def _silu(x):
    return x * _sigmoid(x)


def _softplus(x):
    return jnp.maximum(x, 0.0) + jnp.log1p(jnp.exp(-jnp.abs(x)))


def _rms(x, w):
    return x * lax.rsqrt(jnp.mean(x * x, axis=-1, keepdims=True) + EPS) * w


def _norm_proj_kernel(x_ref, nw_ref, w_ref, o_ref, h_ref):
    @pl.when(pl.program_id(1) == 0)
    def _():
        h_ref[...] = _rms(x_ref[...], nw_ref[...]).astype(BF16)

    o_ref[...] = _dot(h_ref[...], w_ref[...])


def _norm_proj(x, nw, w, tm):
    m = x.shape[0]
    tn = 1408
    return pl.pallas_call(
        _norm_proj_kernel,
        grid=(m // tm, P_WIDTH // tn),
        in_specs=[pl.BlockSpec((tm, D_MODEL), lambda i, j: (i, 0)),
                  pl.BlockSpec((1, D_MODEL), lambda i, j: (0, 0)),
                  pl.BlockSpec((D_MODEL, tn), lambda i, j: (0, j))],
        out_specs=pl.BlockSpec((tm, tn), lambda i, j: (i, j)),
        out_shape=jax.ShapeDtypeStruct((m, P_WIDTH), F32),
        scratch_shapes=[pltpu.VMEM((tm, D_MODEL), BF16)],
        compiler_params=_cparams(("parallel", "arbitrary")),
        name="norm_proj",
    )(x, nw, w)


def _conv_from_ext(ext_ref, g, cw, c, width):
    acc = None
    for k in range(width):
        off = SUBLANES - (width - 1) + k
        term = cw[k:k + 1, :] * ext_ref[g, off:off + c, :]
        acc = term if acc is None else acc + term
    return acc


def _tri_masks(c):
    r = lax.broadcasted_iota(jnp.int32, (c, c), 0)
    s = lax.broadcasted_iota(jnp.int32, (c, c), 1)
    return r >= s, r > s


def _split2(x):
    hi = x.astype(BF16)
    return hi, (x - hi.astype(F32)).astype(BF16)


def _split3(x):
    a = x.astype(BF16)
    r = x - a.astype(F32)
    b = r.astype(BF16)
    return a, b, (r - b.astype(F32)).astype(BF16)


def _dot3(a2, b2, dot=None):
    dot = _dot if dot is None else dot
    return dot(a2[0], b2[0]) + (dot(a2[0], b2[1]) + dot(a2[1], b2[0]))


def _mask_dot(mask, x3):
    return _dot(mask, x3[0]) + (_dot(mask, x3[1]) + _dot(mask, x3[2]))


def _mask_dot_rhs(x3, mask):
    return _dot(x3[0], mask) + (_dot(x3[1], mask) + _dot(x3[2], mask))


def _mask_dot_tn(x3, mask):
    return _dot_tn(x3[0], mask) + (_dot_tn(x3[1], mask) + _dot_tn(x3[2], mask))


def _unit_lower_inverses(n_list, c):
    r = lax.broadcasted_iota(jnp.int32, (c, c), 0)
    s = lax.broadcasted_iota(jnp.int32, (c, c), 1)
    eye = jnp.where(r == s, 1.0, 0.0).astype(F32)
    ts = [eye - n for n in n_list]
    p2 = [_split2(-n) for n in n_list]
    span = 2
    while span < c:
        p2 = [_split2(_dot3(q, q)) for q in p2]
        ts = [t + _dot3(_split2(t), q) for t, q in zip(ts, p2)]
        span *= 2
    return ts


def _gdn_kernel(q_ref, k_ref, v_ref, z_ref, sm_ref, cw_ref, par_ref, nw_ref, cs_ref, s0_ref,
                o_ref, so_ref, co_ref, ext_ref, s_ref, *, G, C, L, NC):
    ci = pl.program_id(1)
    hist = GDN_CONV - 1

    @pl.when(ci == 0)
    def _():
        s_ref[...] = s0_ref[...]
        ext_ref[...] = jnp.zeros(ext_ref.shape, F32)
        for g in range(G):
            ext_ref[g, SUBLANES - hist:SUBLANES, :] = cs_ref[g]

    causal, strict = _tri_masks(C)
    tril = jnp.where(causal, 1.0, 0.0).astype(BF16)
    triu = jnp.where(lax.broadcasted_iota(jnp.int32, (C, C), 0)
                     <= lax.broadcasted_iota(jnp.int32, (C, C), 1), 1.0, 0.0).astype(BF16)
    rows = lax.broadcasted_iota(jnp.int32, (C, 1), 0)
    valid = rows < L
    lane = lax.broadcasted_iota(jnp.int32, (1, LANES), 1)
    dt_bias = par_ref[0:1, :]
    neg_a = -jnp.exp(par_ref[1:2, :])
    cw = cw_ref[...]
    nw = nw_ref[...]

    hs = []
    for g in range(G):
        ext_ref[g, SUBLANES:SUBLANES + C, 0:GDN_KEY] = q_ref[g]
        ext_ref[g, SUBLANES:SUBLANES + C, GDN_KEY:2 * GDN_KEY] = k_ref[g]
        ext_ref[g, SUBLANES:SUBLANES + C, 2 * GDN_KEY:] = v_ref[g]
        xc = _silu(_conv_from_ext(ext_ref, g, cw, C, GDN_CONV))
        new_hist = ext_ref[g, SUBLANES + L - hist:SUBLANES + L, :]
        co_ref[g] = new_hist
        ext_ref[g, SUBLANES - hist:SUBLANES, :] = new_hist

        small = sm_ref[g]
        gmat = jnp.where(valid & (lane < SM_B), neg_a * _softplus(small + dt_bias), 0.0)
        bmat = jnp.where(valid, _sigmoid(small), 0.0)
        g3 = _split3(gmat)
        gc = _mask_dot(tril, g3)
        gct = _mask_dot_tn(g3, triu)
        z = z_ref[g]
        for h in range(GDN_HEADS):
            gcol = gc[:, SM_A + h:SM_A + h + 1]
            grow = gct[SM_A + h:SM_A + h + 1, :]
            glast = gc[C - 1:C, SM_A + h:SM_A + h + 1]
            bcol = bmat[:, SM_B + h:SM_B + h + 1]
            qh = xc[:, h * GDN_DK:(h + 1) * GDN_DK]
            kh = xc[:, GDN_KEY + h * GDN_DK:GDN_KEY + (h + 1) * GDN_DK]
            vh = xc[:, 2 * GDN_KEY + h * GDN_DV:2 * GDN_KEY + (h + 1) * GDN_DV]
            qh = qh * lax.rsqrt(jnp.sum(qh * qh, axis=-1, keepdims=True) + EPS) * (GDN_DK ** -0.5)
            kh = jnp.where(valid, kh * lax.rsqrt(jnp.sum(kh * kh, axis=-1, keepdims=True) + EPS), 0.0)
            vh = jnp.where(valid, vh, 0.0)
            ecol = jnp.exp(gcol)
            hs.append(dict(
                g=g, h=h, bcol=bcol, q=qh, k=kh, kb=kh.astype(BF16), glast=glast,
                gam=jnp.where(causal, jnp.exp(jnp.minimum(gcol - grow, 0.0)), 0.0),
                rhs=jnp.concatenate([vh * bcol, kh * (bcol * ecol)], axis=1),
                q_dec=(qh * ecol).astype(BF16), k_dec=(kh * jnp.exp(glast - gcol)).astype(BF16),
                z=z[:, h * GDN_DV:(h + 1) * GDN_DV], s=s_ref[g, h]))

    kks = [_dot_nt(d["kb"], d["kb"]) for d in hs]
    tinvs = _unit_lower_inverses([jnp.where(strict, d["bcol"] * kk * d["gam"], 0.0) for d, kk in zip(hs, kks)], C)
    us = [_dot3(_split2(t), _split2(d["rhs"])) for d, t in zip(hs, tinvs)]
    qks = [(_dot_nt(d["q"].astype(BF16), d["kb"]) * d["gam"]).astype(BF16) for d in hs]
    sbs = [d["s"].astype(BF16) for d in hs]
    ws = [u[:, :GDN_DV] - _dot(u[:, GDN_DV:].astype(BF16), sb) for u, sb in zip(us, sbs)]
    wbs = [w.astype(BF16) for w in ws]
    outs = [_dot(d["q_dec"], sb) + _dot(qk, wb) for d, sb, qk, wb in zip(hs, sbs, qks, wbs)]
    s_new = [d["s"] * jnp.exp(d["glast"]) + _dot_tn(d["k_dec"], wb) for d, wb in zip(hs, wbs)]
    outs = [_rms(o, nw) * _silu(d["z"]) for d, o in zip(hs, outs)]
    o_ref[...] = jnp.stack([jnp.concatenate(outs[g * GDN_HEADS:(g + 1) * GDN_HEADS], axis=1)
                            for g in range(G)]).astype(o_ref.dtype)
    s_ref[...] = jnp.stack(s_new).reshape(s_ref.shape)

    @pl.when(ci == NC - 1)
    def _():
        so_ref[...] = s_ref[...]


def _gdn(p3, conv_w, par, nw, conv_state, state, layer, *, G, C, L):
    b, t, _ = p3.shape
    nc = t // C
    blk = lambda col, w: pl.BlockSpec((G, C, w), lambda i, c: (i, c, col // w))
    fixed = lambda shape: pl.BlockSpec(shape, lambda i, c: (0,) * len(shape))
    kern = functools.partial(_gdn_kernel, G=G, C=C, L=L, NC=nc)
    return pl.pallas_call(
        kern,
        grid=(b // G, nc),
        in_specs=[blk(COL_GQ, GDN_KEY), blk(COL_GK, GDN_KEY), blk(COL_GV, GDN_VAL), blk(COL_GZ, GDN_VAL),
                  blk(COL_SMALL, LANES),
                  fixed((GDN_CONV, GDN_CONV_DIM)), fixed((SUBLANES, LANES)), fixed((1, GDN_DV)),
                  pl.BlockSpec((None, G, GDN_CONV - 1, GDN_CONV_DIM), lambda i, c: (layer, i, 0, 0)),
                  pl.BlockSpec((None, G, GDN_HEADS, GDN_DK, GDN_DV), lambda i, c: (layer, i, 0, 0, 0))],
        out_specs=[pl.BlockSpec((G, C, GDN_VAL), lambda i, c: (i, c, 0)),
                   pl.BlockSpec((G, GDN_HEADS, GDN_DK, GDN_DV), lambda i, c: (i, 0, 0, 0)),
                   pl.BlockSpec((G, GDN_CONV - 1, GDN_CONV_DIM), lambda i, c: (i, 0, 0))],
        out_shape=[jax.ShapeDtypeStruct((b, t, GDN_VAL), F32),
                   jax.ShapeDtypeStruct((b, GDN_HEADS, GDN_DK, GDN_DV), F32),
                   jax.ShapeDtypeStruct((b, GDN_CONV - 1, GDN_CONV_DIM), F32)],
        scratch_shapes=[pltpu.VMEM((G, C + SUBLANES, GDN_CONV_DIM), F32),
                        pltpu.VMEM((G, GDN_HEADS, GDN_DK, GDN_DV), F32)],
        compiler_params=_cparams(("parallel", "arbitrary")),
        name="gdn",
    )(p3, p3, p3, p3, p3, conv_w, par, nw, conv_state, state)


def _ssd_kernel(x_ref, b_ref, c_ref, z_ref, sm_ref, cw_ref, cb_ref, par_ref, dexp_ref, nw_ref, e_ref,
                cs_ref, h0_ref, o_ref, ho_ref, co_ref, ext_ref, h_ref, *, G, C, L, NC):
    ci = pl.program_id(1)
    hist = SSM_CONV - 1
    nbc = SSM_GROUPS * SSM_DSTATE

    @pl.when(ci == 0)
    def _():
        h_ref[...] = h0_ref[...].reshape(h_ref.shape)
        ext_ref[...] = jnp.zeros(ext_ref.shape, F32)
        for g in range(G):
            ext_ref[g, SUBLANES - hist:SUBLANES, :] = cs_ref[g]

    causal, _ = _tri_masks(C)
    tril = jnp.where(causal, 1.0, 0.0).astype(BF16)
    triu = jnp.where(lax.broadcasted_iota(jnp.int32, (C, C), 0)
                     <= lax.broadcasted_iota(jnp.int32, (C, C), 1), 1.0, 0.0).astype(BF16)
    rows = lax.broadcasted_iota(jnp.int32, (C, 1), 0)
    valid = rows < L
    lane = lax.broadcasted_iota(jnp.int32, (1, LANES), 1)
    dt_lanes = (lane >= SM_DT) & (lane < SM_DT + SSM_HEADS)
    half = [jnp.where(lane < SSM_HEADDIM, 1.0, 0.0).astype(F32), jnp.where(lane >= SSM_HEADDIM, 1.0, 0.0).astype(F32)]
    dt_bias = par_ref[0:1, :]
    neg_a = -jnp.exp(par_ref[1:2, :])
    cw = cw_ref[...]
    cb = cb_ref[...]
    expand = e_ref[...]
    gw = SSM_INNER // SSM_GROUPS

    pre = []
    for g in range(G):
        ext_ref[g, SUBLANES:SUBLANES + C, 0:SSM_INNER] = x_ref[g]
        ext_ref[g, SUBLANES:SUBLANES + C, SSM_INNER:SSM_INNER + nbc] = b_ref[g]
        ext_ref[g, SUBLANES:SUBLANES + C, SSM_INNER + nbc:] = c_ref[g]
        xbc = _silu(_conv_from_ext(ext_ref, g, cw, C, SSM_CONV) + cb)
        new_hist = ext_ref[g, SUBLANES + L - hist:SUBLANES + L, :]
        co_ref[g] = new_hist
        ext_ref[g, SUBLANES - hist:SUBLANES, :] = new_hist

        dt = jnp.where(valid & dt_lanes, _softplus(sm_ref[g] + dt_bias), 0.0)
        la_exp = _mask_dot_rhs(_split3(dt * neg_a), expand)
        dt_exp = _mask_dot_rhs(_split3(dt), expand)
        la3 = _split3(la_exp)
        acum = _mask_dot(tril, la3)
        acum_t = _mask_dot_tn(la3, triu)
        x = xbc[:, :SSM_INNER]
        xdt = x * dt_exp
        pre.append(dict(xbc=xbc, x=x, xdt=xdt, acum=acum, acum_t=acum_t, cin=jnp.exp(acum),
                        xdec=(xdt * jnp.exp(acum[C - 1:C, :] - acum)).astype(BF16)))

    cbs = []
    for g in range(G):
        xbc = pre[g]["xbc"]
        for gi in range(SSM_GROUPS):
            bg = xbc[:, SSM_INNER + gi * SSM_DSTATE:SSM_INNER + (gi + 1) * SSM_DSTATE].astype(BF16)
            cg = xbc[:, SSM_INNER + nbc + gi * SSM_DSTATE:SSM_INNER + nbc + (gi + 1) * SSM_DSTATE].astype(BF16)
            cbs.append((bg, cg, _dot_nt(cg, bg)))

    ys, h_new = [], []
    for g in range(G):
        d = pre[g]
        for p in range(SSM_HEADS // 2):
            bg, cg, cbm = cbs[g * SSM_GROUPS + p // 2]
            ps = slice(p * LANES, (p + 1) * LANES)
            xdt_p = d["xdt"][:, ps]
            hp = h_ref[g, p]
            y_p = _dot_nt(cg, hp.astype(BF16)) * d["cin"][:, ps]
            for e in range(2):
                head = 2 * p + e
                acol = d["acum"][:, head * SSM_HEADDIM:head * SSM_HEADDIM + 1]
                arow = d["acum_t"][head * SSM_HEADDIM:head * SSM_HEADDIM + 1, :]
                seg = jnp.where(causal, jnp.exp(jnp.minimum(acol - arow, 0.0)), 0.0)
                y_p = y_p + _dot((cbm * seg).astype(BF16), (xdt_p * half[e]).astype(BF16))
            ys.append(y_p)
            h_new.append(hp * jnp.exp(d["acum_t"][p * LANES:(p + 1) * LANES, C - 1:C]) + _dot_tn(d["xdec"][:, ps], bg))

    outs = []
    for g in range(G):
        d = pre[g]
        y = jnp.concatenate(ys[g * (SSM_HEADS // 2):(g + 1) * (SSM_HEADS // 2)], axis=1)
        y = (y + dexp_ref[...] * d["x"]) * _silu(z_ref[g])
        outs.append(jnp.concatenate([_rms(y[:, gi * gw:(gi + 1) * gw], nw_ref[:, gi * gw:(gi + 1) * gw])
                                     for gi in range(SSM_GROUPS)], axis=1))
    o_ref[...] = jnp.stack(outs).astype(o_ref.dtype)
    h_ref[...] = jnp.stack(h_new).reshape(h_ref.shape)

    @pl.when(ci == NC - 1)
    def _():
        ho_ref[...] = h_ref[...].reshape(ho_ref.shape)


def _ssd(p3, conv_w, conv_b, par, dexp, nw, expand, conv_state, state, layer, *, G, C, L):
    b, t, _ = p3.shape
    nc = t // C
    nbc = SSM_GROUPS * SSM_DSTATE
    blk = lambda col, w: pl.BlockSpec((G, C, w), lambda i, c: (i, c, col // w))
    fixed = lambda shape: pl.BlockSpec(shape, lambda i, c: (0,) * len(shape))
    kern = functools.partial(_ssd_kernel, G=G, C=C, L=L, NC=nc)
    return pl.pallas_call(
        kern,
        grid=(b // G, nc),
        in_specs=[blk(COL_SX, SSM_INNER), blk(COL_SB, nbc), blk(COL_SC, nbc), blk(COL_SZ, SSM_INNER),
                  blk(COL_SMALL, LANES),
                  fixed((SSM_CONV, SSM_CONV_DIM)), fixed((1, SSM_CONV_DIM)), fixed((SUBLANES, LANES)),
                  fixed((1, SSM_INNER)), fixed((1, SSM_INNER)), fixed((LANES, SSM_INNER)),
                  pl.BlockSpec((None, G, SSM_CONV - 1, SSM_CONV_DIM), lambda i, c: (layer, i, 0, 0)),
                  pl.BlockSpec((None, G, SSM_HEADS, SSM_HEADDIM, SSM_DSTATE), lambda i, c: (layer, i, 0, 0, 0))],
        out_specs=[pl.BlockSpec((G, C, SSM_INNER), lambda i, c: (i, c, 0)),
                   pl.BlockSpec((G, SSM_HEADS, SSM_HEADDIM, SSM_DSTATE), lambda i, c: (i, 0, 0, 0)),
                   pl.BlockSpec((G, SSM_CONV - 1, SSM_CONV_DIM), lambda i, c: (i, 0, 0))],
        out_shape=[jax.ShapeDtypeStruct((b, t, SSM_INNER), F32),
                   jax.ShapeDtypeStruct((b, SSM_HEADS, SSM_HEADDIM, SSM_DSTATE), F32),
                   jax.ShapeDtypeStruct((b, SSM_CONV - 1, SSM_CONV_DIM), F32)],
        scratch_shapes=[pltpu.VMEM((G, C + SUBLANES, SSM_CONV_DIM), F32),
                        pltpu.VMEM((G, SSM_HEADS // 2, LANES, SSM_DSTATE), F32)],
        compiler_params=_cparams(("parallel", "arbitrary")),
        name="ssd",
    )(p3, p3, p3, p3, p3, conv_w, conv_b, par, dexp, nw, expand, conv_state, state)


def _split_dot(a, b_bf16):
    hi = a.astype(BF16)
    lo = (a - hi.astype(F32)).astype(BF16)
    return _dot(hi, b_bf16) + _dot(lo, b_bf16)


def _qknorm_kernel(q_ref, k_ref, v_ref, qw_ref, kw_ref, ones_ref, oq_ref, ok_ref, km_ref, qt_ref, kb_ref, vt_ref):
    ones = ones_ref[...]
    q = q_ref[...]
    k = k_ref[...]
    inv = 1.0 / ATTN_HD
    qn = q * lax.rsqrt(_split_dot(q * q, ones) * inv + EPS) * qw_ref[...]
    oq_ref[...] = qn
    kn = k * lax.rsqrt(_split_dot(k * k, ones) * inv + EPS) * kw_ref[...]
    ok_ref[...] = kn
    km_ref[0] = jnp.mean(kn, axis=0, keepdims=True)
    qt_ref[...] = (qn * (ATTN_HD ** -0.5)).T
    kb_ref[0] = kn.astype(BF16)
    vt_ref[0] = v_ref[...].T.astype(BF16)


def _qknorm(p2, qw, kw, ones, tm):
    m = p2.shape[0]
    nt = m // tm
    col = lambda c: pl.BlockSpec((tm, ATTN_WIDTH), lambda i: (i, c // ATTN_WIDTH))
    return pl.pallas_call(
        _qknorm_kernel,
        grid=(nt,),
        in_specs=[col(COL_AQ), col(COL_AK), col(COL_AV),
                  pl.BlockSpec((1, ATTN_WIDTH), lambda i: (0, 0)),
                  pl.BlockSpec((1, ATTN_WIDTH), lambda i: (0, 0)),
                  pl.BlockSpec((ATTN_WIDTH, ATTN_WIDTH), lambda i: (0, 0))],
        out_specs=[pl.BlockSpec((tm, ATTN_WIDTH), lambda i: (i, 0)),
                   pl.BlockSpec((tm, ATTN_WIDTH), lambda i: (i, 0)),
                   pl.BlockSpec((1, 1, ATTN_WIDTH), lambda i: (i, 0, 0)),
                   pl.BlockSpec((ATTN_WIDTH, tm), lambda i: (0, i)),
                   pl.BlockSpec((1, tm, ATTN_WIDTH), lambda i: (i, 0, 0)),
                   pl.BlockSpec((1, ATTN_WIDTH, tm), lambda i: (i, 0, 0))],
        out_shape=[jax.ShapeDtypeStruct((m, ATTN_WIDTH), F32),
                   jax.ShapeDtypeStruct((m, ATTN_WIDTH), F32),
                   jax.ShapeDtypeStruct((nt, 1, ATTN_WIDTH), F32),
                   jax.ShapeDtypeStruct((ATTN_WIDTH, m), F32),
                   jax.ShapeDtypeStruct((nt, tm, ATTN_WIDTH), BF16),
                   jax.ShapeDtypeStruct((nt, ATTN_WIDTH, tm), BF16)],
        compiler_params=_cparams(("parallel",)),
        name="qknorm",
    )(p2, p2, p2, qw, kw, ones)


def _top_blocks_neg(sc, n_valid):
    blk = lax.broadcasted_iota(jnp.int32, (sc.shape[0], 1), 0)
    sc = jnp.where(blk < n_valid, sc, NEG)
    sel = jnp.zeros(sc.shape, F32)
    for _ in range(MOBA_TOPK):
        mx = jnp.max(sc, axis=0, keepdims=True)
        idx = jnp.min(jnp.where(sc == mx, blk, sc.shape[0]), axis=0, keepdims=True)
        pick = blk == idx
        sel = jnp.where(pick, 1.0, sel)
        sc = jnp.where(pick, LOWEST, sc)
    return jnp.where((blk < n_valid) & (sel > 0.0), 0.0, NEG)


def _moba_kernel(qt_ref, kb_ref, vt_ref, km_ref, bias_ref, o_ref,
                 qm_ref, sel_ref, m_ref, l_ref, acc_ref, lg_ref):
    jq = pl.program_id(1)
    row = lax.broadcasted_iota(jnp.int32, (LANES, 1), 0)
    half = [row < ATTN_HD, row >= ATTN_HD]

    qt = qt_ref[...]
    km = km_ref[0]
    for h in range(ATTN_HEADS):
        ps = slice((h // 2) * LANES, (h // 2 + 1) * LANES)
        qm = jnp.where(half[h % 2], qt[ps, :], 0.0)
        qm_ref[h] = qm.astype(BF16)
        sel_ref[h] = _top_blocks_neg(_dot3(_split2(km[:, ps]), _split2(qm)), jq)
    m_ref[...] = jnp.full(m_ref.shape, NEG, F32)
    l_ref[...] = jnp.zeros(l_ref.shape, F32)
    acc_ref[...] = jnp.zeros(acc_ref.shape, F32)
    ones = jnp.ones((2 * SUBLANES, MOBA_BLOCK), BF16)

    def key_block(n, carry, far):
        d = jq - n
        kb = kb_ref[n]
        vt = vt_ref[n]
        m_old = m_ref[...]
        l_old = l_ref[...]
        m_new = []
        for h in range(ATTN_HEADS):
            ps = slice((h // 2) * LANES, (h // 2 + 1) * LANES)
            if far:
                lg = _dot(kb[:, ps], qm_ref[h]) + (sel_ref[h, pl.ds(n, 1), :] + bias_ref[h, BIAS_ND - 1, 0:1, 0:1])
            else:
                rowneg = jnp.where(d == 0, 0.0, sel_ref[h, pl.ds(n, 1), :])
                lg = _dot(kb[:, ps], qm_ref[h]) + bias_ref[h, d] + rowneg
            lg_ref[h] = lg
            m_new.append(jnp.maximum(m_old[h:h + 1, :], jnp.max(lg, axis=0, keepdims=True)))
        l_new, acc_new = [], []
        for h in range(ATTN_HEADS):
            alpha = jnp.exp(m_old[h:h + 1, :] - m_new[h])
            pt = jnp.exp(lg_ref[h] - m_new[h]).astype(BF16)
            lhs = jnp.concatenate([vt[h * ATTN_HD:(h + 1) * ATTN_HD, :], ones], axis=0)
            r = _dot(lhs, pt)
            l_new.append(l_old[h:h + 1, :] * alpha + r[ATTN_HD:ATTN_HD + 1, :])
            acc_new.append(acc_ref[h] * alpha + r[0:ATTN_HD, :])
        m_ref[...] = jnp.concatenate(m_new, axis=0)
        l_ref[...] = jnp.concatenate(l_new, axis=0)
        acc_ref[...] = jnp.stack(acc_new)
        return carry

    n_far = jnp.maximum(jq + 1 - (BIAS_ND - 1), 0)
    lax.fori_loop(0, n_far, functools.partial(key_block, far=True), 0)
    lax.fori_loop(n_far, jq + 1, functools.partial(key_block, far=False), 0)
    l = l_ref[...]
    ot = jnp.concatenate([acc_ref[h] / l[h:h + 1, :] for h in range(ATTN_HEADS)], axis=0)
    o_ref[0] = ot.T.astype(o_ref.dtype)


def _moba_prompt(qt, kb, vt, kmean, bias_tiles, b):
    nb = kb.shape[0] // b
    t = nb * MOBA_BLOCK
    once = pl.Buffered(1)
    return pl.pallas_call(
        _moba_kernel,
        grid=(b, nb),
        in_specs=[pl.BlockSpec((ATTN_WIDTH, MOBA_BLOCK), lambda i, j: (0, i * nb + j)),
                  pl.BlockSpec((nb, MOBA_BLOCK, ATTN_WIDTH), lambda i, j: (i, 0, 0), pipeline_mode=once),
                  pl.BlockSpec((nb, ATTN_WIDTH, MOBA_BLOCK), lambda i, j: (i, 0, 0), pipeline_mode=once),
                  pl.BlockSpec((1, LANES, ATTN_WIDTH), lambda i, j: (i, 0, 0)),
                  pl.BlockSpec((ATTN_HEADS, BIAS_ND, MOBA_BLOCK, MOBA_BLOCK), lambda i, j: (0, 0, 0, 0),
                               pipeline_mode=once)],
        out_specs=pl.BlockSpec((1, MOBA_BLOCK, ATTN_WIDTH), lambda i, j: (i, j, 0)),
        out_shape=jax.ShapeDtypeStruct((b, t, ATTN_WIDTH), F32),
        scratch_shapes=[pltpu.VMEM((ATTN_HEADS, LANES, MOBA_BLOCK), BF16),
                        pltpu.VMEM((ATTN_HEADS, LANES, MOBA_BLOCK), F32),
                        pltpu.VMEM((ATTN_HEADS, MOBA_BLOCK), F32),
                        pltpu.VMEM((ATTN_HEADS, MOBA_BLOCK), F32),
                        pltpu.VMEM((ATTN_HEADS, ATTN_HD, MOBA_BLOCK), F32),
                        pltpu.VMEM((ATTN_HEADS, MOBA_BLOCK, MOBA_BLOCK), F32)],
        compiler_params=_cparams(("parallel", "arbitrary")),
        name="moba_prompt",
    )(qt, kb, vt, kmean, bias_tiles)


def _moba_sample_kernel(pt_ref, q_ref, kn_ref, vn_ref, bias_ref, bown_ref, hm_ref, pick_ref, ck_hbm, cv_hbm,
                        o_ref, kbuf, vbuf, sem, *, layer, npages, page, nseq, nq, spg):
    b = pl.program_id(0)
    slot = lax.rem(b, 2)
    past = npages * page
    nsteps = nseq // spg

    def page_copies(step, sl):
        out = []
        for r in range(spg):
            for p in range(npages):
                pg = pt_ref[step * spg + r, p]
                rows = pl.ds(r * past + p * page, page)
                out.append(pltpu.make_async_copy(ck_hbm.at[layer, pg], kbuf.at[sl, rows], sem.at[0, sl]))
                out.append(pltpu.make_async_copy(cv_hbm.at[layer, pg], vbuf.at[sl, rows], sem.at[1, sl]))
        return out

    @pl.when(b == 0)
    def _():
        for cp in page_copies(0, 0):
            cp.start()

    @pl.when(b + 1 < nsteps)
    def _():
        for cp in page_copies(b + 1, 1 - slot):
            cp.start()

    for cp in page_copies(b, slot):
        cp.wait()

    nblk = past // MOBA_BLOCK
    nrep = ATTN_HEADS * nq
    hm = hm_ref[...]
    blk = lax.broadcasted_iota(jnp.int32, (nblk, 1), 0)
    ones = jnp.ones((past, LANES), BF16)

    def one_sequence(r):
        q = q_ref[r] * (ATTN_HD ** -0.5)
        qrep = _dot_tn(pick_ref[...], q, HI) * hm
        kp = kbuf[slot, r * past:(r + 1) * past, :]
        vp = vbuf[slot, r * past:(r + 1) * past, :]
        kmean = jnp.mean(kp.reshape(nblk, MOBA_BLOCK, ATTN_WIDTH), axis=1)
        sc = _dot_nt(kmean, qrep, HI)
        sel = jnp.zeros(sc.shape, F32)
        for _ in range(min(MOBA_TOPK, nblk)):
            mx = jnp.max(sc, axis=0, keepdims=True)
            idx = jnp.min(jnp.where(sc == mx, blk, nblk), axis=0, keepdims=True)
            pick = blk == idx
            sel = jnp.where(pick, 1.0, sel)
            sc = jnp.where(pick, LOWEST, sc)
        selk = jnp.broadcast_to(sel[:, None, :], (nblk, MOBA_BLOCK, nrep)).reshape(past, nrep) > 0.0

        qb = qrep.astype(BF16)
        lg = jnp.where(selk, _dot_nt(kp.astype(BF16), qb) + bias_ref[...], NEG)
        lo = _dot_nt(kn_ref[r].astype(BF16), qb) + bown_ref[...]
        m = jnp.maximum(jnp.max(lg, axis=0, keepdims=True), jnp.max(lo, axis=0, keepdims=True))
        pp = jnp.where(selk, jnp.exp(lg - m), 0.0).astype(BF16)
        po = jnp.exp(lo - m).astype(BF16)
        den = _dot_tn(pp, ones) + _dot_tn(po, ones[0:SUBLANES])
        out = _dot_tn(pp, vp.astype(BF16)) + _dot_tn(po, vn_ref[r].astype(BF16))
        out = out / den[:, 0:1] * hm
        return _dot(pick_ref[...], out, HI)

    o_ref[...] = jnp.stack([one_sequence(r) for r in range(spg)]).astype(o_ref.dtype)


def _moba_sample(page_table, aq, ak, av_p3, bias_t, bias_own, head_mask, pick, cache_k, cache_v, layer, nq):
    nseq = aq.shape[0]
    npages = page_table.shape[1]
    page = cache_k.shape[2]
    past = npages * page
    nrep = ATTN_HEADS * nq
    spg = 2 if nseq % 2 == 0 else 1
    kern = functools.partial(_moba_sample_kernel, layer=layer, npages=npages, page=page, nseq=nseq, nq=nq, spg=spg)
    gs = pltpu.PrefetchScalarGridSpec(
        num_scalar_prefetch=1,
        grid=(nseq // spg,),
        in_specs=[pl.BlockSpec((spg, SUBLANES, ATTN_WIDTH), lambda i, pt: (i, 0, 0)),
                  pl.BlockSpec((spg, SUBLANES, ATTN_WIDTH), lambda i, pt: (i, 0, 0)),
                  pl.BlockSpec((spg, SUBLANES, ATTN_WIDTH), lambda i, pt: (i, 0, COL_AV // ATTN_WIDTH)),
                  pl.BlockSpec((past, nrep), lambda i, pt: (0, 0)),
                  pl.BlockSpec((SUBLANES, nrep), lambda i, pt: (0, 0)),
                  pl.BlockSpec((nrep, ATTN_WIDTH), lambda i, pt: (0, 0)),
                  pl.BlockSpec((SUBLANES, nrep), lambda i, pt: (0, 0)),
                  pl.BlockSpec(memory_space=pl.ANY),
                  pl.BlockSpec(memory_space=pl.ANY)],
        out_specs=pl.BlockSpec((spg, SUBLANES, ATTN_WIDTH), lambda i, pt: (i, 0, 0)),
        scratch_shapes=[pltpu.VMEM((2, spg * past, ATTN_WIDTH), F32),
                        pltpu.VMEM((2, spg * past, ATTN_WIDTH), F32),
                        pltpu.SemaphoreType.DMA((2, 2))])
    return pl.pallas_call(
        kern,
        grid_spec=gs,
        out_shape=jax.ShapeDtypeStruct((nseq, SUBLANES, ATTN_WIDTH), F32),
        compiler_params=_cparams(("arbitrary",)),
        name="moba_sample",
    )(page_table, aq, ak, av_p3, bias_t, bias_own, head_mask, pick, cache_k, cache_v)


def _merge_kernel(x_ref, og_ref, oa_ref, os_ref, g0_ref, g1_ref, g2_ref, wg_ref, wa_ref, ws_ref, wo_ref, o_ref):
    merged = (_sigmoid(g0_ref[...]) * _dot(og_ref[...].astype(BF16), wg_ref[...])
              + _sigmoid(g1_ref[...]) * _dot(oa_ref[...].astype(BF16), wa_ref[...])
              + _sigmoid(g2_ref[...]) * _dot(os_ref[...].astype(BF16), ws_ref[...]))
    o_ref[...] = x_ref[...] + _dot(merged.astype(BF16), wo_ref[...])


def _merge(x, o_gdn, o_att, o_ssm, p2, wg, wa, ws, wo, tm):
    m = x.shape[0]
    row = lambda w, cb=0: pl.BlockSpec((tm, w), lambda i: (i, cb))
    fixed = lambda a, bb: pl.BlockSpec((a, bb), lambda i: (0, 0))
    return pl.pallas_call(
        _merge_kernel,
        grid=(m // tm,),
        in_specs=[row(D_MODEL), row(GDN_VAL), row(ATTN_WIDTH), row(SSM_INNER),
                  row(D_MODEL, COL_G0 // D_MODEL), row(D_MODEL, COL_G1 // D_MODEL), row(D_MODEL, COL_G2 // D_MODEL),
                  fixed(GDN_VAL, D_MODEL), fixed(ATTN_WIDTH, D_MODEL), fixed(SSM_INNER, D_MODEL),
                  fixed(D_MODEL, D_MODEL)],
        out_specs=row(D_MODEL),
        out_shape=jax.ShapeDtypeStruct((m, D_MODEL), F32),
        compiler_params=_cparams(("parallel",)),
        name="merge",
    )(x, o_gdn, o_att, o_ssm, p2, p2, p2, wg, wa, ws, wo)


FFN_HALO = 16


def _ffn_prompt_kernel(x_ref, xh_ref, nw_ref, upg_ref, upv_ref, cwg_ref, cwv_ref, cbg_ref, cbv_ref, dn_ref,
                       o_ref, tg_ref, tv_ref, h_ref, ug_ref, uv_ref, *, tm, seq_tiles):
    i = pl.program_id(0)
    j = pl.program_id(1)

    @pl.when(j == 0)
    def _():
        x = x_ref[...]
        h_ref[FFN_HALO:, :] = _rms(x, nw_ref[...]).astype(BF16)
        h_ref[0:FFN_HALO, :] = _rms(xh_ref[...], nw_ref[...]).astype(BF16)
        o_ref[...] = x

    first = lax.rem(i, seq_tiles) == 0
    rows = lax.broadcasted_iota(jnp.int32, (tm + FFN_HALO, 1), 0)
    keep = jnp.where(first & (rows < FFN_HALO), 0.0, 1.0)
    h = h_ref[...]
    ug_ref[...] = _dot(h, upg_ref[...]) * keep
    uv_ref[...] = _dot(h, upv_ref[...]) * keep

    def conv(u_ref, cw_ref, cb_ref):
        acc = cb_ref[...]
        for k in range(FFN_CONV):
            off = FFN_HALO - (FFN_CONV - 1) + k
            acc = acc + cw_ref[k:k + 1, :] * u_ref[off:off + tm, :]
        return acc

    act = _silu(conv(ug_ref, cwg_ref, cbg_ref)) * conv(uv_ref, cwv_ref, cbv_ref)
    o_ref[...] += _dot(act.astype(BF16), dn_ref[...])
    tg_ref[0] = ug_ref[tm + FFN_HALO - SUBLANES:, :]
    tv_ref[0] = uv_ref[tm + FFN_HALO - SUBLANES:, :]


def _ffn_prompt(x, nw, up, cw, cb, dn, tm, seq_len):
    m = x.shape[0]
    nt = m // tm
    nj = D_FF // FFN_TN
    hb = tm // FFN_HALO
    kern = functools.partial(_ffn_prompt_kernel, tm=tm, seq_tiles=seq_len // tm)
    vec = lambda r, off: pl.BlockSpec((r, FFN_TN), lambda i, j: (0, j + off))
    return pl.pallas_call(
        kern,
        grid=(nt, nj),
        in_specs=[pl.BlockSpec((tm, D_MODEL), lambda i, j: (i, 0)),
                  pl.BlockSpec((FFN_HALO, D_MODEL), lambda i, j: (jnp.maximum(i * hb - 1, 0), 0)),
                  pl.BlockSpec((1, D_MODEL), lambda i, j: (0, 0)),
                  pl.BlockSpec((D_MODEL, FFN_TN), lambda i, j: (0, j)),
                  pl.BlockSpec((D_MODEL, FFN_TN), lambda i, j: (0, j + nj)),
                  vec(FFN_CONV, 0), vec(FFN_CONV, nj), vec(1, 0), vec(1, nj),
                  pl.BlockSpec((FFN_TN, D_MODEL), lambda i, j: (j, 0))],
        out_specs=[pl.BlockSpec((tm, D_MODEL), lambda i, j: (i, 0)),
                   pl.BlockSpec((1, SUBLANES, FFN_TN), lambda i, j: (i, 0, j)),
                   pl.BlockSpec((1, SUBLANES, FFN_TN), lambda i, j: (i, 0, j))],
        out_shape=[jax.ShapeDtypeStruct((m, D_MODEL), F32),
                   jax.ShapeDtypeStruct((nt, SUBLANES, D_FF), F32),
                   jax.ShapeDtypeStruct((nt, SUBLANES, D_FF), F32)],
        scratch_shapes=[pltpu.VMEM((tm + FFN_HALO, D_MODEL), BF16),
                        pltpu.VMEM((tm + FFN_HALO, FFN_TN), F32),
                        pltpu.VMEM((tm + FFN_HALO, FFN_TN), F32)],
        compiler_params=_cparams(("parallel", "arbitrary")),
        name="ffn_prompt",
    )(x, x, nw, up, up, cw, cw, cb, cb, dn)


def _ffn_sample_kernel(x_ref, nw_ref, upg_ref, upv_ref, cwg_ref, cwv_ref, cbg_ref, cbv_ref, dn_ref, sg_ref, sv_ref,
                       o_ref, ng_ref, nv_ref, h_ref, *, nb):
    j = pl.program_id(0)

    @pl.when(j == 0)
    def _():
        x = x_ref[...]
        h_ref[...] = _rms(x, nw_ref[...]).astype(BF16)
        o_ref[...] = x

    h = h_ref[...]
    hist = FFN_CONV - 1

    def conv(u, s_ref, cw_ref, cb_ref, n_ref):
        full = jnp.concatenate([s_ref[...], u], axis=0)
        rows = u.shape[0]
        acc = cb_ref[...]
        for k in range(FFN_CONV):
            acc = acc + cw_ref[k:k + 1, :] * full[k * nb:k * nb + rows, :]
        n_ref[...] = full[rows:, :]
        return acc

    cg = conv(_dot(h, upg_ref[...]), sg_ref, cwg_ref, cbg_ref, ng_ref)
    cv = conv(_dot(h, upv_ref[...]), sv_ref, cwv_ref, cbv_ref, nv_ref)
    o_ref[...] += _dot((_silu(cg) * cv).astype(BF16), dn_ref[...])


def _ffn_sample(x, nw, up, cw, cb, dn, state_g, state_v, nb):
    m = x.shape[0]
    nj = D_FF // FFN_TN
    srows = (FFN_CONV - 1) * nb
    kern = functools.partial(_ffn_sample_kernel, nb=nb)
    vec = lambda r, off: pl.BlockSpec((r, FFN_TN), lambda j: (0, j + off))
    return pl.pallas_call(
        kern,
        grid=(nj,),
        in_specs=[pl.BlockSpec((m, D_MODEL), lambda j: (0, 0)),
                  pl.BlockSpec((1, D_MODEL), lambda j: (0, 0)),
                  pl.BlockSpec((D_MODEL, FFN_TN), lambda j: (0, j)),
                  pl.BlockSpec((D_MODEL, FFN_TN), lambda j: (0, j + nj)),
                  vec(FFN_CONV, 0), vec(FFN_CONV, nj), vec(1, 0), vec(1, nj),
                  pl.BlockSpec((FFN_TN, D_MODEL), lambda j: (j, 0)),
                  pl.BlockSpec((srows, FFN_TN), lambda j: (0, j)),
                  pl.BlockSpec((srows, FFN_TN), lambda j: (0, j))],
        out_specs=[pl.BlockSpec((m, D_MODEL), lambda j: (0, 0)),
                   pl.BlockSpec((srows, FFN_TN), lambda j: (0, j)),
                   pl.BlockSpec((srows, FFN_TN), lambda j: (0, j))],
        out_shape=[jax.ShapeDtypeStruct((m, D_MODEL), F32),
                   jax.ShapeDtypeStruct((srows, D_FF), F32),
                   jax.ShapeDtypeStruct((srows, D_FF), F32)],
        scratch_shapes=[pltpu.VMEM((m, D_MODEL), BF16)],
        compiler_params=_cparams(("arbitrary",)),
        name="ffn_sample",
    )(x, nw, up, up, cw, cw, cb, cb, dn, state_g, state_v)


def _t5_bucket(rel):
    n = jnp.maximum(rel, 0)
    exact = NUM_BUCKETS // 2
    nf = jnp.maximum(n, 1).astype(F32)
    large = exact + (jnp.log(nf / exact) / math.log(MAX_DISTANCE / exact) * (NUM_BUCKETS - exact)).astype(jnp.int32)
    return jnp.where(n < exact, n, jnp.minimum(large, NUM_BUCKETS - 1))


def _bias_by_distance(rel_bias, max_rel):
    return rel_bias.astype(F32)[_t5_bucket(jnp.arange(max_rel)), :].T


def _prompt_bias_tiles(rel_bias):
    blk = MOBA_BLOCK
    tab = _bias_by_distance(rel_bias, (BIAS_ND + 1) * blk)
    tab = jnp.concatenate([jnp.zeros((ATTN_HEADS, blk), F32), tab], axis=1)
    causal = np.arange(blk)[:, None] <= np.arange(blk)[None, :]
    tiles = []
    for d in range(BIAS_ND):
        w = tab[:, d * blk + 1:(d + 2) * blk]
        w = jnp.concatenate([w, jnp.zeros((ATTN_HEADS, 1), F32)], axis=1)
        rows = jnp.tile(w, (1, blk))[:, :blk * (2 * blk - 1)].reshape(ATTN_HEADS, blk, 2 * blk - 1)
        tile = rows[:, :, blk - 1:]
        tiles.append(jnp.where(causal, tile, NEG) if d == 0 else tile)
    return jnp.stack(tiles, axis=1)


def _sample_bias_tables(rel_bias, past_len, nq):
    tab = _bias_by_distance(rel_bias, past_len + nq)
    kpos = np.arange(past_len)[:, None]
    rel = past_len + np.arange(nq)[None, :] - kpos
    past_t = jnp.transpose(tab[:, rel], (1, 0, 2)).reshape(past_len, ATTN_HEADS * nq)
    ko = np.arange(SUBLANES)[:, None]
    relo = np.arange(nq)[None, :] - ko
    own = jnp.transpose(tab[:, np.maximum(relo, 0)], (1, 0, 2))
    ok = jnp.asarray(((relo >= 0) & (ko < nq))[:, None, :])
    own_t = jnp.where(ok, own, NEG).reshape(SUBLANES, ATTN_HEADS * nq)
    return past_t, own_t


def _permute_w_in(w):
    cuts = np.cumsum((0,) + IN_SIZES)
    sec = lambda k, lo=0, hi=None: w[:, cuts[k] + lo: (cuts[k + 1] if hi is None else cuts[k] + hi)]
    nbc = SSM_GROUPS * SSM_DSTATE
    small = jnp.concatenate([sec(2), sec(3), sec(9),
                             jnp.zeros((w.shape[0], LANES - 2 * GDN_HEADS - SSM_HEADS), w.dtype)], axis=1)
    parts = [sec(7), sec(8, 0, SSM_INNER),
             sec(10, 0, D_MODEL), sec(10, D_MODEL, 2 * D_MODEL), sec(10, 2 * D_MODEL, 3 * D_MODEL),
             sec(0, 0, GDN_KEY), sec(0, GDN_KEY, 2 * GDN_KEY), sec(0, 2 * GDN_KEY, GDN_CONV_DIM), sec(1),
             sec(4), sec(5), sec(6),
             sec(8, SSM_INNER, SSM_INNER + nbc), sec(8, SSM_INNER + nbc, SSM_CONV_DIM), small]
    return jnp.concatenate(parts, axis=1).astype(BF16)


def _lane_row(vals, offset):
    return jnp.zeros((LANES,), F32).at[offset:offset + vals.shape[0]].set(vals.astype(F32))


def _pick_tm(m, cap):
    tm = cap
    while m % tm:
        tm //= 2
    return tm


def kernel(x_prompt, x_sample, cache_k, cache_v, page_table, state_gdn, state_gdn_conv, state_ssm, state_ssm_conv, state_ffn_conv, rel_bias, norm_mix, w_in, gdn_conv_w, gdn_A_log, gdn_dt_bias, gdn_norm, q_norm, k_norm, ssm_conv_w, ssm_conv_b, ssm_A_log, ssm_dt_bias, ssm_D, ssm_norm, w_br_gdn, w_br_attn, w_br_ssm, w_out, norm_ffn, ffn_up, ffn_conv_w, ffn_conv_b, ffn_down):
    depth = w_in.shape[0]
    bp, seq, _ = x_prompt.shape
    bd, nq, _ = x_sample.shape
    npool, page = cache_k.shape[1], cache_k.shape[2]
    npages = page_table.shape[1]
    past_len = npages * page
    nblk = seq // MOBA_BLOCK
    assert seq % MOBA_BLOCK == 0 and nblk <= LANES and past_len % MOBA_BLOCK == 0
    assert nq <= SUBLANES and bd % SUBLANES == 0

    ck = cache_k.reshape(depth, npool, page, ATTN_WIDTH)
    cv = cache_v.reshape(depth, npool, page, ATTN_WIDTH)
    bias_tiles = _prompt_bias_tiles(rel_bias)
    bias_past, bias_own = _sample_bias_tables(rel_bias, past_len, nq)
    nrep = ATTN_HEADS * nq
    col_head = np.arange(nrep) // nq
    head_mask = jnp.asarray((np.arange(ATTN_WIDTH)[None, :] // ATTN_HD == col_head[:, None]).astype(np.float32))
    pick = jnp.asarray((np.arange(SUBLANES)[:, None] == (np.arange(nrep) % nq)[None, :]).astype(np.float32))
    ones_bd = jnp.asarray((np.arange(ATTN_WIDTH)[:, None] // ATTN_HD
                           == np.arange(ATTN_WIDTH)[None, :] // ATTN_HD).astype(np.float32)).astype(BF16)
    expand = jnp.asarray((np.arange(LANES)[:, None] - SM_DT
                          == np.arange(SSM_INNER)[None, :] // SSM_HEADDIM).astype(np.float32)).astype(BF16)
    zeros_gdn_state = jnp.zeros((1, bp, GDN_HEADS, GDN_DK, GDN_DV), F32)
    zeros_gdn_conv = jnp.zeros((1, bp, GDN_CONV - 1, GDN_CONV_DIM), F32)
    zeros_ssm_state = jnp.zeros((1, bp, SSM_HEADS, SSM_HEADDIM, SSM_DSTATE), F32)
    zeros_ssm_conv = jnp.zeros((1, bp, SSM_CONV - 1, SSM_CONV_DIM), F32)

    mp = bp * seq
    tm_p = _pick_tm(mp, 1024)
    tm_ffn = _pick_tm(seq, 512)
    chunk = math.gcd(seq, 64)
    ms = bd * SUBLANES
    yp = x_prompt.reshape(mp, D_MODEL)
    ys = x_sample
    out_p = [[] for _ in range(7)]
    out_s = [[] for _ in range(7)]

    for l in range(depth):
        w_perm = _permute_w_in(w_in[l])
        nmix = norm_mix[l].reshape(1, D_MODEL)
        gdn_par = jnp.zeros((SUBLANES, LANES), F32).at[0].set(_lane_row(gdn_dt_bias[l], SM_A)).at[1].set(
            _lane_row(gdn_A_log[l], SM_A))
        ssm_par = jnp.zeros((SUBLANES, LANES), F32).at[0].set(_lane_row(ssm_dt_bias[l], SM_DT)).at[1].set(
            _lane_row(ssm_A_log[l], SM_DT))
        gnw = gdn_norm[l].reshape(1, GDN_DV)
        qw = jnp.tile(q_norm[l], ATTN_HEADS).reshape(1, ATTN_WIDTH)
        kw = jnp.tile(k_norm[l], ATTN_HEADS).reshape(1, ATTN_WIDTH)
        scb = ssm_conv_b[l].reshape(1, SSM_CONV_DIM)
        dexp = jnp.repeat(ssm_D[l], SSM_HEADDIM).reshape(1, SSM_INNER)
        snw = ssm_norm[l].reshape(1, SSM_INNER)
        wg, wa, ws, wo = (w_br_gdn[l].astype(BF16), w_br_attn[l].astype(BF16), w_br_ssm[l].astype(BF16),
                          w_out[l].astype(BF16))
        nffn = norm_ffn[l].reshape(1, D_MODEL)
        up = ffn_up[l].astype(BF16)
        dn = ffn_down[l].astype(BF16)
        fcw = ffn_conv_w[l]
        fcb = ffn_conv_b[l].reshape(1, 2 * D_FF)

        p2 = _norm_proj(yp, nmix, w_perm, tm_p)
        p3 = p2.reshape(bp, seq, P_WIDTH)
        og, gs_new, gc_new = _gdn(p3, gdn_conv_w[l], gdn_par, gnw, zeros_gdn_conv, zeros_gdn_state, 0,
                                  G=bp, C=chunk, L=chunk)
        osm, hs_new, sc_new = _ssd(p3, ssm_conv_w[l], scb, ssm_par, dexp, snw, expand, zeros_ssm_conv,
                                   zeros_ssm_state, 0, G=bp, C=chunk, L=chunk)
        _, ak, km, qt, kb, vt = _qknorm(p2, qw, kw, ones_bd, MOBA_BLOCK)
        kmean = jnp.pad(km.reshape(bp, nblk, ATTN_WIDTH), ((0, 0), (0, LANES - nblk), (0, 0)))
        oa = _moba_prompt(qt, kb, vt, kmean, bias_tiles, bp)
        x1 = _merge(yp, og.reshape(mp, GDN_VAL), oa.reshape(mp, ATTN_WIDTH), osm.reshape(mp, SSM_INNER), p2,
                    wg, wa, ws, wo, _pick_tm(mp, 512))
        yp, tg, tv = _ffn_prompt(x1, nffn, up, fcw, fcb, dn, tm_ffn, seq)
        tpb = seq // tm_ffn
        last = np.arange(bp) * tpb + tpb - 1
        ffn_state = jnp.concatenate([tg[last], tv[last]], axis=-1)[:, SUBLANES - (FFN_CONV - 1):, :]
        for acc, tns in zip(out_p, (ak.reshape(bp, seq, ATTN_HEADS, ATTN_HD),
                                    p3[:, :, COL_AV:COL_AV + ATTN_WIDTH].reshape(bp, seq, ATTN_HEADS, ATTN_HD),
                                    gs_new, gc_new, hs_new, sc_new, ffn_state)):
            acc.append(tns)

        xs_pad = jnp.pad(ys, ((0, 0), (0, SUBLANES - nq), (0, 0))).reshape(ms, D_MODEL)
        q2 = _norm_proj(xs_pad, nmix, w_perm, _pick_tm(ms, 1024))
        q3 = q2.reshape(bd, SUBLANES, P_WIDTH)
        gsz = 4
        og_s, gs_s, gc_s = _gdn(q3, gdn_conv_w[l], gdn_par, gnw, state_gdn_conv, state_gdn, l,
                                G=gsz, C=SUBLANES, L=nq)
        os_s, hs_s, sc_s = _ssd(q3, ssm_conv_w[l], scb, ssm_par, dexp, snw, expand, state_ssm_conv, state_ssm, l,
                                G=gsz, C=SUBLANES, L=nq)
        aq_s, ak_s, _, _, _, _ = _qknorm(q2, qw, kw, ones_bd, _pick_tm(ms, MOBA_BLOCK))
        oa_s = _moba_sample(page_table, aq_s.reshape(bd, SUBLANES, ATTN_WIDTH), ak_s.reshape(bd, SUBLANES, ATTN_WIDTH),
                            q3, bias_past, bias_own, head_mask, pick, ck, cv, l, nq)
        x1_s = _merge(xs_pad, og_s.reshape(ms, GDN_VAL), oa_s.reshape(ms, ATTN_WIDTH), os_s.reshape(ms, SSM_INNER), q2,
                      wg, wa, ws, wo, _pick_tm(ms, 512))
        x1_t = jnp.transpose(x1_s.reshape(bd, SUBLANES, D_MODEL)[:, :nq], (1, 0, 2)).reshape(nq * bd, D_MODEL)
        st = jnp.transpose(state_ffn_conv[l], (1, 0, 2)).reshape((FFN_CONV - 1) * bd, 2 * D_FF)
        y_t, ng, nv = _ffn_sample(x1_t, nffn, up, fcw, fcb, dn, st[:, :D_FF], st[:, D_FF:], bd)
        ys = jnp.transpose(y_t.reshape(nq, bd, D_MODEL), (1, 0, 2))
        ffn_state_s = jnp.transpose(jnp.concatenate([ng, nv], axis=-1).reshape(FFN_CONV - 1, bd, 2 * D_FF), (1, 0, 2))
        for acc, tns in zip(out_s, (ak_s.reshape(bd, SUBLANES, ATTN_HEADS, ATTN_HD)[:, :nq],
                                    q3[:, :nq, COL_AV:COL_AV + ATTN_WIDTH].reshape(bd, nq, ATTN_HEADS, ATTN_HD),
                                    gs_s, gc_s, hs_s, sc_s, ffn_state_s)):
            acc.append(tns)

    k_p, v_p, gdn_p, gdn_conv_p, ssm_p, ssm_conv_p, ffn_conv_p = [jnp.stack(a) for a in out_p]
    k_s, v_s, gdn_s, gdn_conv_s, ssm_s, ssm_conv_s, ffn_conv_s = [jnp.stack(a) for a in out_s]
    return (yp.reshape(bp, seq, D_MODEL), ys, k_p, v_p, gdn_p, gdn_conv_p, ssm_p, ssm_conv_p, ffn_conv_p,
            k_s, v_s, gdn_s, gdn_conv_s, ssm_s, ssm_conv_s, ffn_conv_s)
```
